```python
import math
import jax
import jax.numpy as jnp
from jax import lax
import numpy as np

D_MODEL = 4096
BATCH = 1
SEQ = 16384
DEPTH = 4

N_MIXERS = 3
N_S5 = (DEPTH + 2) // 3
N_POOL = (DEPTH + 1) // 3
N_FOX = DEPTH // 3

D_FF = 5632
FFN_RES_WEIGHT = 0.5
RMS_EPS = 1e-6

S5_GROUP = 16
S5_STATE = 64
S5_GROUPS = D_MODEL // S5_GROUP
S5_DT_MIN = 1e-3
S5_DT_MAX = 1e-1

POOL_WINDOWS = (2, 4, 8, 16)
POOL_GROUP = D_MODEL // len(POOL_WINDOWS)

FOX_HEAD_DIM = 128
FOX_HEADS = D_MODEL // FOX_HEAD_DIM
FOX_BLOCK = 128

kernel_name = 'hybrid_s5_pool_fox_macaron'


def rmsnorm(x, g):
    xf = x.astype(jnp.float32)
    y = xf * lax.rsqrt(jnp.mean(xf * xf, axis=-1, keepdims=True) + RMS_EPS)
    return (y * g.astype(jnp.float32)).astype(x.dtype)


def swiglu(h, w_gate, w_up, w_down):
    return (jax.nn.silu(h @ w_gate) * (h @ w_up)) @ w_down


def _ssm_combine(left, right):
    a_l, b_l = left
    a_r, b_r = right
    return a_r * a_l, a_r * b_l + b_r


def s5_mixer(h, w_in, lam_re, lam_im, log_step, b_re, b_im, c_re, c_im, d_skip, w_glu, w_out):
    bsz, seq, _ = h.shape
    f32 = jnp.float32
    u = (h @ w_in).astype(f32).reshape(bsz, seq, S5_GROUPS, S5_GROUP)
    lam = lax.complex(lam_re.astype(f32), lam_im.astype(f32))
    step = jnp.exp(log_step.astype(f32))[:, None]
    a_bar = jnp.exp(lam * step)
    b = lax.complex(b_re.astype(f32), b_im.astype(f32))
    b_bar = ((a_bar - 1.0) / lam)[..., None] * b
    bu = jnp.einsum('gpn,blgn->lbgp', b_bar, u.astype(jnp.complex64))
    a_seq = jnp.broadcast_to(a_bar, bu.shape)
    _, states = lax.associative_scan(_ssm_combine, (a_seq, bu), axis=0)
    c = lax.complex(c_re.astype(f32), c_im.astype(f32))
    y = jnp.einsum('gnp,lbgp->blgn', c, states).real
    y = y + d_skip.astype(f32).reshape(S5_GROUPS, S5_GROUP) * u
    y = y.reshape(bsz, seq, D_MODEL).astype(h.dtype)
    z = jax.nn.gelu(y)
    z = z * jax.nn.sigmoid(z @ w_glu)
    return z @ w_out


def pool_mixer(h, w_in, w_group, scale, w_out):
    seq = h.shape[1]
    uf = (h @ w_in).astype(jnp.float32)
    count = jnp.arange(1, seq + 1, dtype=jnp.float32)[None, :, None]
    outs = []
    for gi, win in enumerate(POOL_WINDOWS):
        ug = uf[..., gi * POOL_GROUP:(gi + 1) * POOL_GROUP]
        cs = jnp.cumsum(ug, axis=1)
        cs_lag = jnp.pad(cs, ((0, 0), (win, 0), (0, 0)))[:, :seq]
        mean = (cs - cs_lag) / jnp.minimum(count, float(win))
        outs.append((mean - ug).astype(h.dtype) @ w_group[gi])
    mixed = jnp.concatenate(outs, axis=-1) * scale
    return mixed @ w_out


def fox_mixer(h, w_in, b_f, w_out):
    bsz, seq, _ = h.shape
    proj = h @ w_in
    q = proj[..., :D_MODEL].reshape(bsz, seq, FOX_HEADS, FOX_HEAD_DIM)
    k = proj[..., D_MODEL:2 * D_MODEL].reshape(bsz, seq, FOX_HEADS, FOX_HEAD_DIM)
    v = proj[..., 2 * D_MODEL:3 * D_MODEL].reshape(bsz, seq, FOX_HEADS, FOX_HEAD_DIM)
    log_f = jax.nn.log_sigmoid(proj[..., 3 * D_MODEL:].astype(jnp.float32) + b_f.astype(jnp.float32))
    cum = jnp.cumsum(log_f, axis=1).transpose(0, 2, 1)
    n_blk = seq // FOX_BLOCK
    q_blk = q.reshape(bsz, n_blk, FOX_BLOCK, FOX_HEADS, FOX_HEAD_DIM).transpose(1, 0, 2, 3, 4)
    cum_q_blk = cum.reshape(bsz, FOX_HEADS, n_blk, FOX_BLOCK).transpose(2, 0, 1, 3)
    starts = jnp.arange(n_blk, dtype=jnp.int32) * FOX_BLOCK
    key_pos = jnp.arange(seq, dtype=jnp.int32)
    sm_scale = FOX_HEAD_DIM ** -0.5

    def attend_block(args):
        q_i, cq_i, start = args
        s = jnp.einsum('bqhd,bkhd->bhqk', q_i, k, preferred_element_type=jnp.float32) * sm_scale
        s = s + cq_i[..., :, None] - cum[:, :, None, :]
        causal = (start + jnp.arange(FOX_BLOCK, dtype=jnp.int32))[:, None] >= key_pos[None, :]
        s = jnp.where(causal, s, -jnp.inf)
        p = jax.nn.softmax(s, axis=-1)
        return jnp.einsum('bhqk,bkhd->bqhd', p.astype(v.dtype), v)

    o = lax.map(attend_block, (q_blk, cum_q_blk, starts))
    o = o.transpose(1, 0, 2, 3, 4).reshape(bsz, seq, D_MODEL)
    return o @ w_out


def setup_inputs(seed: int = 0) -> dict:
    key = jax.random.key(seed)
    keys = iter(jax.random.split(key, 64))

    def normal(shape, scale):
        return jax.random.normal(next(keys), shape, jnp.float32) * scale

    def gain(shape):
        return 1.0 + normal(shape, 0.02)

    d, f = D_MODEL, D_FF
    g, p, n = S5_GROUPS, S5_STATE, S5_GROUP
    lam_im0 = jnp.pi * jnp.arange(p, dtype=jnp.float32)
    return {
        'x': normal((BATCH, SEQ, d), 1.0),
        'norm_ffn1': gain((DEPTH, d)),
        'ffn1_w_gate': normal((DEPTH, d, f), d ** -0.5),
        'ffn1_w_up': normal((DEPTH, d, f), d ** -0.5),
        'ffn1_w_down': normal((DEPTH, f, d), f ** -0.5),
        'norm_mix': gain((DEPTH, d)),
        's5_w_in': normal((N_S5, d, d), d ** -0.5),
        's5_lam_re': -0.5 + normal((N_S5, g, p), 0.01),
        's5_lam_im': lam_im0 + normal((N_S5, g, p), 0.01),
        's5_log_step': jax.random.uniform(next(keys), (N_S5, g), jnp.float32,
                                          math.log(S5_DT_MIN), math.log(S5_DT_MAX)),
        's5_b_re': normal((N_S5, g, p, n), (2 * n) ** -0.5),
        's5_b_im': normal((N_S5, g, p, n), (2 * n) ** -0.5),
        's5_c_re': normal((N_S5, g, n, p), (2 * p) ** -0.5),
        's5_c_im': normal((N_S5, g, n, p), (2 * p) ** -0.5),
        's5_d': normal((N_S5, d), 1.0),
        's5_w_glu': normal((N_S5, d, d), d ** -0.5),
        's5_w_out': normal((N_S5, d, d), d ** -0.5),
        'pool_w_in': normal((N_POOL, d, d), d ** -0.5),
        'pool_w_group': normal((N_POOL, len(POOL_WINDOWS), POOL_GROUP, POOL_GROUP), POOL_GROUP ** -0.5),
        'pool_scale': gain((N_POOL, d)),
        'pool_w_out': normal((N_POOL, d, d), d ** -0.5),
        'fox_w_in': normal((N_FOX, d, 3 * d + FOX_HEADS), d ** -0.5),
        'fox_b_f': jnp.linspace(1.0, 6.0, FOX_HEADS, dtype=jnp.float32)[None, :] + normal((N_FOX, FOX_HEADS), 0.1),
        'fox_w_out': normal((N_FOX, d, d), d ** -0.5),
        'norm_ffn2': gain((DEPTH, d)),
        'ffn2_w_gate': normal((DEPTH, d, f), d ** -0.5),
        'ffn2_w_up': normal((DEPTH, d, f), d ** -0.5),
        'ffn2_w_down': normal((DEPTH, f, d), f ** -0.5),
        'norm_final': gain((d,)),
    }


def reference(x, norm_ffn1, ffn1_w_gate, ffn1_w_up, ffn1_w_down, norm_mix,
              s5_w_in, s5_lam_re, s5_lam_im, s5_log_step, s5_b_re, s5_b_im, s5_c_re, s5_c_im,
              s5_d, s5_w_glu, s5_w_out, pool_w_in, pool_w_group, pool_scale, pool_w_out,
              fox_w_in, fox_b_f, fox_w_out, norm_ffn2, ffn2_w_gate, ffn2_w_up, ffn2_w_down,
              norm_final):
    for i in range(DEPTH):
        x = x + FFN_RES_WEIGHT * swiglu(rmsnorm(x, norm_ffn1[i]), ffn1_w_gate[i], ffn1_w_up[i], ffn1_w_down[i])
        h = rmsnorm(x, norm_mix[i])
        j = i // N_MIXERS
        kind = i % N_MIXERS
        if kind == 0:
            m = s5_mixer(h, s5_w_in[j], s5_lam_re[j], s5_lam_im[j], s5_log_step[j], s5_b_re[j], s5_b_im[j],
                         s5_c_re[j], s5_c_im[j], s5_d[j], s5_w_glu[j], s5_w_out[j])
        elif kind == 1:
            m = pool_mixer(h, pool_w_in[j], pool_w_group[j], pool_scale[j], pool_w_out[j])
        else:
            m = fox_mixer(h, fox_w_in[j], fox_b_f[j], fox_w_out[j])
        x = x + m
        x = x + FFN_RES_WEIGHT * swiglu(rmsnorm(x, norm_ffn2[i]), ffn2_w_gate[i], ffn2_w_up[i], ffn2_w_down[i])
    return rmsnorm(x, norm_final)
```

```python
import functools
import math

import jax
import jax.numpy as jnp
from jax import lax
from jax.experimental import pallas as pl
from jax.experimental.pallas import tpu as pltpu

RMS_EPS = 1e-6
FFN_RES_WEIGHT = 0.5
N_MIXERS = 3
S5_GROUP = 16
S5_CHUNK = 16
POOL_WINDOWS = (2, 4, 8, 16)
FOX_HEAD_DIM = 128

V7X_LANES = 128
V7X_SUBLANES = 8
V7X_VMEM_BYTES = 64 * 2**20
VMEM_LIMIT = V7X_VMEM_BYTES - 8 * 2**20

F32 = jnp.float32
BF16 = jnp.bfloat16


def _pick(dim, pref, align=V7X_LANES):
    if dim <= pref:
        return dim
    t = (pref // align) * align
    while t >= align:
        if dim % t == 0:
            return t
        t -= align
    return dim


def _params(*semantics):
    return pltpu.CompilerParams(dimension_semantics=semantics, vmem_limit_bytes=VMEM_LIMIT)


def _rmsnorm_body(x_ref, g_ref, o_ref):
    x = x_ref[...]
    ms = jnp.mean(x * x, axis=-1, keepdims=True)
    o_ref[...] = (x * lax.rsqrt(ms + RMS_EPS) * g_ref[...]).astype(o_ref.dtype)


def _rmsnorm(x, g, out_dtype):
    rows, d = x.shape
    tr = _pick(rows, 256)
    return pl.pallas_call(
        _rmsnorm_body,
        grid=(rows // tr,),
        in_specs=[pl.BlockSpec((tr, d), lambda i: (i, 0)),
                  pl.BlockSpec((1, d), lambda i: (0, 0))],
        out_specs=pl.BlockSpec((tr, d), lambda i: (i, 0)),
        out_shape=jax.ShapeDtypeStruct((rows, d), out_dtype),
        compiler_params=_params("parallel"),
        name="rmsnorm",
    )(x, g.reshape(1, d).astype(F32))


def _mm_body(*refs, n_b, n_e, nk, epilogue):
    a_ref = refs[0]
    b_refs = refs[1:1 + n_b]
    e_refs = refs[1 + n_b:1 + n_b + n_e]
    o_ref = refs[1 + n_b + n_e]
    acc_refs = refs[2 + n_b + n_e:]
    a = a_ref[...]
    parts = [jnp.dot(a, b[...], preferred_element_type=F32) for b in b_refs]

    def finish(vals):
        o_ref[...] = epilogue(vals, [e[...] for e in e_refs]).astype(o_ref.dtype)

    if nk == 1:
        finish(parts)
        return
    k = pl.program_id(2)

    @pl.when(k == 0)
    def _():
        for acc, p in zip(acc_refs, parts):
            acc[...] = p

    @pl.when(jnp.logical_and(k > 0, k < nk - 1))
    def _():
        for acc, p in zip(acc_refs, parts):
            acc[...] += p

    @pl.when(k == nk - 1)
    def _():
        finish([acc[...] + p for acc, p in zip(acc_refs, parts)])


def _matmul(a, bs, extras, epilogue, out_dtype, *, tm, tn, tk, name):
    m, kdim = a.shape
    n = bs[0].shape[1]
    tm, tn, tk = _pick(m, tm), _pick(n, tn), _pick(kdim, tk)
    nk = kdim // tk
    in_specs = [pl.BlockSpec((tm, tk), lambda i, j, k: (i, k))]
    in_specs += [pl.BlockSpec((tk, tn), lambda i, j, k: (k, j)) for _ in bs]
    for _, kind in extras:
        if kind == "mn":
            in_specs.append(pl.BlockSpec((tm, tn), lambda i, j, k: (i, j)))
        else:
            in_specs.append(pl.BlockSpec((1, tn), lambda i, j, k: (0, j)))
    scratch = [pltpu.VMEM((tm, tn), F32) for _ in bs] if nk > 1 else []
    return pl.pallas_call(
        functools.partial(_mm_body, n_b=len(bs), n_e=len(extras), nk=nk, epilogue=epilogue),
        grid=(m // tm, n // tn, nk),
        in_specs=in_specs,
        out_specs=pl.BlockSpec((tm, tn), lambda i, j, k: (i, j)),
        out_shape=jax.ShapeDtypeStruct((m, n), out_dtype),
        scratch_shapes=scratch,
        compiler_params=_params("parallel", "parallel", "arbitrary"),
        name=name,
    )(a, *bs, *[e for e, _ in extras])


def _ep_identity(parts, extras):
    return parts[0]


def _ep_colscale(parts, extras):
    return parts[0] * extras[0]


def _ep_swiglu(parts, extras):
    gate, up = parts
    return gate * (1.0 / (1.0 + jnp.exp(-gate))) * up


def _ep_residual(weight, parts, extras):
    return extras[0] + weight * parts[0]


def _ep_glu(parts, extras):
    z = extras[0].astype(F32)
    return z * (1.0 / (1.0 + jnp.exp(-parts[0])))


def _gelu_tanh(x):
    c = math.sqrt(2.0 / math.pi)
    return x * (0.5 * (1.0 + jnp.tanh(c * (x + 0.044715 * (x * x * x)))))


def _ffn(x, h, w_gate, w_up, w_down):
    act = _matmul(h, [w_gate.astype(BF16), w_up.astype(BF16)], [], _ep_swiglu, BF16,
                  tm=1024, tn=512, tk=4096, name="ffn_up")
    return _matmul(act, [w_down.astype(BF16)], [(x, "mn")],
                   functools.partial(_ep_residual, FFN_RES_WEIGHT), F32,
                   tm=512, tn=1024, tk=2816, name="ffn_down")


def _proj_residual(x, a, w, name):
    return _matmul(a, [w.astype(BF16)], [(x, "mn")], functools.partial(_ep_residual, 1.0), F32,
                   tm=1024, tn=1024, tk=4096, name=name)


def _s5_operators(lam_re, lam_im, log_step, b_re, b_im, c_re, c_im, d_skip):
    g, p = lam_re.shape
    n, t = S5_GROUP, S5_CHUNK
    hi = lax.Precision.HIGHEST
    step = jnp.exp(log_step.astype(F32))[:, None]
    er, ei = lam_re.astype(F32) * step, lam_im.astype(F32) * step
    ks = jnp.arange(t + 1, dtype=F32)[:, None, None]
    mag = jnp.exp(er[None] * ks)
    pr, pi = mag * jnp.cos(ei[None] * ks), mag * jnp.sin(ei[None] * ks)
    xr, xi = pr[1] - 1.0, pi[1]
    den = lam_re * lam_re + lam_im * lam_im
    qr, qi = (xr * lam_re + xi * lam_im) / den, (xi * lam_re - xr * lam_im) / den
    bb_re = qr[..., None] * b_re - qi[..., None] * b_im
    bb_im = qr[..., None] * b_im + qi[..., None] * b_re
    ca_re = c_re[None] * pr[:, :, None, :] - c_im[None] * pi[:, :, None, :]
    ca_im = c_re[None] * pi[:, :, None, :] + c_im[None] * pr[:, :, None, :]
    kern = (jnp.einsum("kgnp,gpm->kgnm", ca_re[:t], bb_re, precision=hi)
            - jnp.einsum("kgnp,gpm->kgnm", ca_im[:t], bb_im, precision=hi))
    lag = jnp.arange(t)[None, :] - jnp.arange(t)[:, None]
    toep = jnp.where((lag >= 0)[:, :, None, None, None], kern[jnp.maximum(lag, 0)], 0.0)
    m_mat = toep.transpose(2, 0, 4, 1, 3).reshape(g, t * n, t * n)
    d_row = jnp.tile(d_skip.astype(F32).reshape(g, n), (1, t))
    m_mat = m_mat + jnp.eye(t * n, dtype=F32)[None] * d_row[:, None, :]
    rev_r, rev_i = pr[t - 1::-1][:t], pi[t - 1::-1][:t]
    rev_r = rev_r.transpose(1, 0, 2)[:, :, None, :]
    rev_i = rev_i.transpose(1, 0, 2)[:, :, None, :]
    bt_re, bt_im = bb_re.transpose(0, 2, 1)[:, None], bb_im.transpose(0, 2, 1)[:, None]
    v_re = (rev_r * bt_re - rev_i * bt_im).reshape(g, t * n, p)
    v_im = (rev_r * bt_im + rev_i * bt_re).reshape(g, t * n, p)
    v_mat = jnp.concatenate([v_re, v_im], axis=-1)
    v_swap = jnp.concatenate([v_im, v_re], axis=-1)
    w_re = ca_re[1:].transpose(1, 3, 0, 2).reshape(g, p, t * n)
    w_im = ca_im[1:].transpose(1, 3, 0, 2).reshape(g, p, t * n)
    w_mat = jnp.concatenate([w_re, -w_im], axis=1)
    a1 = jnp.concatenate([pr[t], pr[t]], axis=-1)
    a2 = jnp.concatenate([-pi[t], pi[t]], axis=-1)
    return m_mat.astype(BF16), v_mat.astype(BF16), v_swap.astype(BF16), w_mat.astype(BF16), a1, a2


def _s5_core_body(u_ref, m_ref, v_ref, vs_ref, w_ref, a1_ref, a2_ref, z_ref,
                  loc_ref, locs_ref, prev_ref, *, gb, chunks):
    for g in range(gb):
        ug = u_ref[g]
        loc_ref[g * chunks:(g + 1) * chunks, :] = jnp.dot(ug, v_ref[g], preferred_element_type=F32)
        locs_ref[g * chunks:(g + 1) * chunks, :] = jnp.dot(ug, vs_ref[g], preferred_element_type=F32)
    a1, a2 = a1_ref[...], a2_ref[...]

    def step(c, carry):
        s, ss = carry
        rows = pl.ds(c, gb, stride=chunks)
        prev_ref[rows, :] = s
        s_new = a1 * s + a2 * ss + loc_ref[rows, :]
        ss_new = a1 * ss - a2 * s + locs_ref[rows, :]
        return s_new, ss_new

    zero = jnp.zeros((gb, a1.shape[-1]), F32)
    lax.fori_loop(0, chunks, step, (zero, zero))
    for g in range(gb):
        y = jnp.dot(u_ref[g], m_ref[g], preferred_element_type=F32)
        y += jnp.dot(prev_ref[g * chunks:(g + 1) * chunks, :].astype(BF16), w_ref[g],
                     preferred_element_type=F32)
        z_ref[g] = _gelu_tanh(y).astype(z_ref.dtype)


def _s5_core(ug, ops):
    m_mat, v_mat, v_swap, w_mat, a1, a2 = ops
    groups, chunks, width = ug.shape
    state = a1.shape[-1]
    gb = V7X_SUBLANES
    blk3 = lambda d1, d2: pl.BlockSpec((gb, d1, d2), lambda i: (i, 0, 0))
    return pl.pallas_call(
        functools.partial(_s5_core_body, gb=gb, chunks=chunks),
        grid=(groups // gb,),
        in_specs=[blk3(chunks, width), blk3(width, width), blk3(width, state), blk3(width, state),
                  blk3(state, width),
                  pl.BlockSpec((gb, state), lambda i: (i, 0)),
                  pl.BlockSpec((gb, state), lambda i: (i, 0))],
        out_specs=blk3(chunks, width),
        out_shape=jax.ShapeDtypeStruct(ug.shape, BF16),
        scratch_shapes=[pltpu.VMEM((gb * chunks, state), F32) for _ in range(3)],
        compiler_params=_params("parallel"),
        name="s5_core",
    )(ug, m_mat, v_mat, v_swap, w_mat, a1, a2)


def _s5_mixer(x, h, w_in, lam_re, lam_im, log_step, b_re, b_im, c_re, c_im, d_skip, w_glu, w_out):
    seq, d = h.shape
    groups, t, n = d // S5_GROUP, S5_CHUNK, S5_GROUP
    chunks = seq // t
    ops = _s5_operators(lam_re, lam_im, log_step, b_re, b_im, c_re, c_im, d_skip)
    u = _matmul(h, [w_in.astype(BF16)], [], _ep_identity, BF16, tm=1024, tn=1024, tk=4096, name="s5_in")
    ug = u.reshape(chunks, t, groups, n).transpose(2, 0, 1, 3).reshape(groups, chunks, t * n)
    zg = _s5_core(ug, ops)
    z = zg.reshape(groups, chunks, t, n).transpose(1, 2, 0, 3).reshape(seq, d)
    gated = _matmul(z, [w_glu.astype(BF16)], [(z, "mn")], _ep_glu, BF16,
                    tm=1024, tn=1024, tk=4096, name="s5_glu")
    return _proj_residual(x, gated, w_out, "s5_out")


def _pool_band_body(u_ref, o_ref, hist_ref, *, tm, tn, hist, group_cols):
    j, i = pl.program_id(0), pl.program_id(1)

    @pl.when(i == 0)
    def _():
        hist_ref[...] = jnp.zeros_like(hist_ref)

    u = u_ref[...]
    ub = u.astype(BF16)
    ext = jnp.concatenate([hist_ref[...], ub], axis=0)
    gid = (j * tn) // group_cols
    win = jnp.int32(POOL_WINDOWS[0])
    for k, w in enumerate(POOL_WINDOWS[1:], start=1):
        win = jnp.where(gid == k, jnp.int32(w), win)
    lag = (lax.broadcasted_iota(jnp.int32, (tm, hist + tm), 0) + hist
           - lax.broadcasted_iota(jnp.int32, (tm, hist + tm), 1))
    band = jnp.where(lag >= 0, jnp.where(lag < win, 1.0, 0.0), 0.0).astype(BF16)
    wsum = jnp.dot(band, ext, preferred_element_type=F32)
    count = jnp.minimum(i * tm + lax.broadcasted_iota(jnp.int32, (tm, 1), 0) + 1, win).astype(F32)
    o_ref[...] = (wsum / count - u).astype(o_ref.dtype)
    hist_ref[...] = ub[tm - hist:, :]


def _pool_band(u, n_groups):
    seq, d = u.shape
    group_cols = d // n_groups
    tm = _pick(seq, 512)
    tn = _pick(group_cols, 512)
    hist = min(V7X_LANES, tm)
    assert hist >= max(POOL_WINDOWS)
    return pl.pallas_call(
        functools.partial(_pool_band_body, tm=tm, tn=tn, hist=hist, group_cols=group_cols),
        grid=(d // tn, seq // tm),
        in_specs=[pl.BlockSpec((tm, tn), lambda j, i: (i, j))],
        out_specs=pl.BlockSpec((tm, tn), lambda j, i: (i, j)),
        out_shape=jax.ShapeDtypeStruct((seq, d), BF16),
        scratch_shapes=[pltpu.VMEM((hist, tn), BF16)],
        compiler_params=_params("parallel", "arbitrary"),
        name="pool_band",
    )(u)


def _pool_group_body(a_ref, w_ref, s_ref, o_ref):
    o_ref[...] = (jnp.dot(a_ref[...], w_ref[...], preferred_element_type=F32) * s_ref[...]).astype(o_ref.dtype)


def _pool_group(diff, w_group, scale):
    seq, d = diff.shape
    n_groups, gc, _ = w_group.shape
    tm = _pick(seq, 1024)
    return pl.pallas_call(
        _pool_group_body,
        grid=(n_groups, seq // tm),
        in_specs=[pl.BlockSpec((tm, gc), lambda g, i: (i, g)),
                  pl.BlockSpec((None, gc, gc), lambda g, i: (g, 0, 0)),
                  pl.BlockSpec((1, gc), lambda g, i: (0, g))],
        out_specs=pl.BlockSpec((tm, gc), lambda g, i: (i, g)),
        out_shape=jax.ShapeDtypeStruct((seq, d), BF16),
        compiler_params=_params("parallel", "parallel"),
        name="pool_group",
    )(diff, w_group.astype(BF16), scale.reshape(1, d).astype(F32))


def _pool_mixer(x, h, w_in, w_group, scale, w_out):
    u = _matmul(h, [w_in.astype(BF16)], [], _ep_identity, F32, tm=1024, tn=1024, tk=4096, name="pool_in")
    diff = _pool_band(u, len(POOL_WINDOWS))
    mixed = _pool_group(diff, w_group, scale)
    return _proj_residual(x, mixed, w_out, "pool_out")


def _split_bf16(x):
    hi = x.astype(BF16)
    r1 = x - hi.astype(F32)
    mid = r1.astype(BF16)
    lo = (r1 - mid.astype(F32)).astype(BF16)
    return hi, mid, lo


def _forget_cumsum_body(fl_ref, b_ref, cum_ref, cumt_ref, carry_ref, *, tm):
    @pl.when(pl.program_id(0) == 0)
    def _():
        carry_ref[...] = jnp.zeros_like(carry_ref)

    x = fl_ref[...] + b_ref[...]
    logf = -(jnp.maximum(-x, 0.0) + jnp.log1p(jnp.exp(-jnp.abs(x))))
    tri = jnp.where(lax.broadcasted_iota(jnp.int32, (tm, tm), 0)
                    >= lax.broadcasted_iota(jnp.int32, (tm, tm), 1), 1.0, 0.0).astype(BF16)
    cs = carry_ref[...]
    for part in _split_bf16(logf):
        cs = cs + jnp.dot(tri, part, preferred_element_type=F32)
    cum_ref[...] = cs
    cumt_ref[...] = cs.T
    carry_ref[...] = cs[tm - 1:tm, :]


def _forget_cumsum(fl, b_row):
    seq, hp = fl.shape
    tm = _pick(seq, 256)
    return pl.pallas_call(
        functools.partial(_forget_cumsum_body, tm=tm),
        grid=(seq // tm,),
        in_specs=[pl.BlockSpec((tm, hp), lambda i: (i, 0)),
                  pl.BlockSpec((1, hp), lambda i: (0, 0))],
        out_specs=[pl.BlockSpec((tm, hp), lambda i: (i, 0)),
                   pl.BlockSpec((hp, tm), lambda i: (0, i))],
        out_shape=[jax.ShapeDtypeStruct((seq, hp), F32), jax.ShapeDtypeStruct((hp, seq), F32)],
        scratch_shapes=[pltpu.VMEM((1, hp), F32)],
        compiler_params=_params("arbitrary"),
        name="forget_cumsum",
    )(fl, b_row)


def _fox_attn_body(q_ref, k_ref, v_ref, cum_ref, cumt_ref, o_ref, m_ref, l_ref, acc_ref, *, tq):
    head, qi = pl.program_id(0), pl.program_id(1)
    q = q_ref[...]
    lane = lax.broadcasted_iota(jnp.int32, cum_ref.shape, 1)
    cq = jnp.sum(jnp.where(lane == head, cum_ref[...], 0.0), axis=-1, keepdims=True)
    ck_row = head % V7X_SUBLANES
    m_ref[...] = jnp.full_like(m_ref, -jnp.inf)
    l_ref[...] = jnp.zeros_like(l_ref)
    acc_ref[...] = jnp.zeros_like(acc_ref)

    def block(kj, diagonal):
        start = pl.multiple_of(kj * tq, tq)
        kb = k_ref[pl.ds(start, tq), :]
        vb = v_ref[pl.ds(start, tq), :]
        ck = cumt_ref[pl.ds(ck_row, 1), pl.ds(start, tq)]
        s = lax.dot_general(q, kb, (((1,), (1,)), ((), ())), preferred_element_type=F32)
        s = s + cq - ck
        if diagonal:
            causal = (lax.broadcasted_iota(jnp.int32, (tq, tq), 0)
                      >= lax.broadcasted_iota(jnp.int32, (tq, tq), 1))
            s = jnp.where(causal, s, -jnp.inf)
        m_prev = m_ref[...]
        m_new = jnp.maximum(m_prev, jnp.max(s, axis=-1, keepdims=True))
        alpha = jnp.exp(m_prev - m_new)
        p = jnp.exp(s - m_new)
        l_ref[...] = alpha * l_ref[...] + jnp.sum(p, axis=-1, keepdims=True)
        acc_ref[...] = alpha * acc_ref[...] + jnp.dot(p.astype(BF16), vb, preferred_element_type=F32)
        m_ref[...] = m_new

    def off_diagonal(kj, carry):
        block(kj, False)
        return carry

    lax.fori_loop(0, qi, off_diagonal, 0)
    block(qi, True)
    o_ref[...] = (acc_ref[...] / l_ref[...]).astype(o_ref.dtype)


def _fox_attention(qkv, cum, cumt, n_heads, tq):
    seq = qkv.shape[0]
    dh = FOX_HEAD_DIM
    hp = cum.shape[1]
    return pl.pallas_call(
        functools.partial(_fox_attn_body, tq=tq),
        grid=(n_heads, seq // tq),
        in_specs=[pl.BlockSpec((tq, dh), lambda h, i: (i, h)),
                  pl.BlockSpec((seq, dh), lambda h, i: (0, n_heads + h)),
                  pl.BlockSpec((seq, dh), lambda h, i: (0, 2 * n_heads + h)),
                  pl.BlockSpec((tq, hp), lambda h, i: (i, 0)),
                  pl.BlockSpec((V7X_SUBLANES, seq), lambda h, i: (h // V7X_SUBLANES, 0))],
        out_specs=pl.BlockSpec((tq, dh), lambda h, i: (i, h)),
        out_shape=jax.ShapeDtypeStruct((seq, n_heads * dh), BF16),
        scratch_shapes=[pltpu.VMEM((tq, 1), F32), pltpu.VMEM((tq, 1), F32), pltpu.VMEM((tq, dh), F32)],
        compiler_params=_params("parallel", "arbitrary"),
        name="fox_attention",
    )(qkv, qkv, qkv, cum, cumt)


def _fox_mixer(x, h, w_in, b_f, w_out, tq=512):
    seq, d = h.shape
    n_heads = d // FOX_HEAD_DIM
    hp = -(-n_heads // V7X_LANES) * V7X_LANES
    w_qkv = w_in[:, :3 * d].astype(BF16)
    w_f = jnp.pad(w_in[:, 3 * d:], ((0, 0), (0, hp - n_heads))).astype(BF16)
    b_row = jnp.pad(b_f.astype(F32), (0, hp - n_heads)).reshape(1, hp)
    sm_scale = FOX_HEAD_DIM ** -0.5
    col_scale = jnp.concatenate([jnp.full((1, d), sm_scale, F32), jnp.ones((1, 2 * d), F32)], axis=1)
    qkv = _matmul(h, [w_qkv], [(col_scale, "n")], _ep_colscale, BF16,
                  tm=1024, tn=1024, tk=4096, name="fox_qkv")
    fl = _matmul(h, [w_f], [], _ep_identity, F32, tm=1024, tn=V7X_LANES, tk=4096, name="fox_gate")
    cum, cumt = _forget_cumsum(fl, b_row)
    o = _fox_attention(qkv, cum, cumt, n_heads, _pick(seq, tq))
    return _proj_residual(x, o, w_out, "fox_out")


def _trunk(x, p):
    depth = p["norm_ffn1"].shape[0]
    for i in range(depth):
        h = _rmsnorm(x, p["norm_ffn1"][i], BF16)
        x = _ffn(x, h, p["ffn1_w_gate"][i], p["ffn1_w_up"][i], p["ffn1_w_down"][i])
        h = _rmsnorm(x, p["norm_mix"][i], BF16)
        j, kind = divmod(i, N_MIXERS)
        if kind == 0:
            x = _s5_mixer(x, h, p["s5_w_in"][j], p["s5_lam_re"][j], p["s5_lam_im"][j], p["s5_log_step"][j],
                          p["s5_b_re"][j], p["s5_b_im"][j], p["s5_c_re"][j], p["s5_c_im"][j],
                          p["s5_d"][j], p["s5_w_glu"][j], p["s5_w_out"][j])
        elif kind == 1:
            x = _pool_mixer(x, h, p["pool_w_in"][j], p["pool_w_group"][j], p["pool_scale"][j], p["pool_w_out"][j])
        else:
            x = _fox_mixer(x, h, p["fox_w_in"][j], p["fox_b_f"][j], p["fox_w_out"][j])
        h = _rmsnorm(x, p["norm_ffn2"][i], BF16)
        x = _ffn(x, h, p["ffn2_w_gate"][i], p["ffn2_w_up"][i], p["ffn2_w_down"][i])
    return _rmsnorm(x, p["norm_final"], F32)


def kernel(x, norm_ffn1, ffn1_w_gate, ffn1_w_up, ffn1_w_down, norm_mix, s5_w_in, s5_lam_re, s5_lam_im, s5_log_step, s5_b_re, s5_b_im, s5_c_re, s5_c_im, s5_d, s5_w_glu, s5_w_out, pool_w_in, pool_w_group, pool_scale, pool_w_out, fox_w_in, fox_b_f, fox_w_out, norm_ffn2, ffn2_w_gate, ffn2_w_up, ffn2_w_down, norm_final):
    p = dict(norm_ffn1=norm_ffn1, ffn1_w_gate=ffn1_w_gate, ffn1_w_up=ffn1_w_up, ffn1_w_down=ffn1_w_down,
             norm_mix=norm_mix, s5_w_in=s5_w_in, s5_lam_re=s5_lam_re, s5_lam_im=s5_lam_im,
             s5_log_step=s5_log_step, s5_b_re=s5_b_re, s5_b_im=s5_b_im, s5_c_re=s5_c_re, s5_c_im=s5_c_im,
             s5_d=s5_d, s5_w_glu=s5_w_glu, s5_w_out=s5_w_out, pool_w_in=pool_w_in, pool_w_group=pool_w_group,
             pool_scale=pool_scale, pool_w_out=pool_w_out, fox_w_in=fox_w_in, fox_b_f=fox_b_f,
             fox_w_out=fox_w_out, norm_ffn2=norm_ffn2, ffn2_w_gate=ffn2_w_gate, ffn2_w_up=ffn2_w_up,
             ffn2_w_down=ffn2_w_down, norm_final=norm_final)
    return jnp.stack([_trunk(x[b], p) for b in range(x.shape[0])])
```

```python
import functools
import math

import jax
import jax.numpy as jnp
from jax import lax
from jax.experimental import pallas as pl
from jax.experimental.pallas import tpu as pltpu

RMS_EPS = 1e-6
FFN_RES_WEIGHT = 0.5
N_MIXERS = 3
S5_GROUP = 16
S5_CHUNK = 16
POOL_WINDOWS = (2, 4, 8, 16)
FOX_HEAD_DIM = 128
LOG2_E = 1.0 / math.log(2.0)

V7X_LANES = 128
V7X_SUBLANES = 8
V7X_VMEM_BYTES = 64 * 2**20
VMEM_LIMIT = V7X_VMEM_BYTES - 8 * 2**20

F32 = jnp.float32
BF16 = jnp.bfloat16


def _pick(dim, pref, align=V7X_LANES):
    if dim <= pref:
        return dim
    t = (pref // align) * align
    while t >= align:
        if dim % t == 0:
            return t
        t -= align
    return dim


def _params(*semantics):
    return pltpu.CompilerParams(dimension_semantics=semantics, vmem_limit_bytes=VMEM_LIMIT)


def _split2(x):
    hi = x.astype(BF16)
    return hi, (x - hi.astype(F32)).astype(BF16)


def _split3(x):
    hi = x.astype(BF16)
    r1 = x - hi.astype(F32)
    mid = r1.astype(BF16)
    return hi, mid, (r1 - mid.astype(F32)).astype(BF16)


def _rmsnorm_body(x_ref, g_ref, o_ref):
    x = x_ref[...]
    ms = jnp.mean(x * x, axis=-1, keepdims=True)
    o_ref[...] = (x * lax.rsqrt(ms + RMS_EPS) * g_ref[...]).astype(o_ref.dtype)


def _rmsnorm(x, g, out_dtype):
    rows, d = x.shape
    tr = _pick(rows, 256)
    return pl.pallas_call(
        _rmsnorm_body,
        grid=(rows // tr,),
        in_specs=[pl.BlockSpec((tr, d), lambda i: (i, 0)),
                  pl.BlockSpec((1, d), lambda i: (0, 0))],
        out_specs=pl.BlockSpec((tr, d), lambda i: (i, 0)),
        out_shape=jax.ShapeDtypeStruct((rows, d), out_dtype),
        compiler_params=_params("parallel"),
        name="rmsnorm",
    )(x, g.reshape(1, d).astype(F32))


def _mm_body(*refs, n_b, n_e, nk, epilogue):
    a_ref = refs[0]
    b_refs = refs[1:1 + n_b]
    e_refs = refs[1 + n_b:1 + n_b + n_e]
    o_ref = refs[1 + n_b + n_e]
    acc_refs = refs[2 + n_b + n_e:]
    a = a_ref[...]
    parts = [jnp.dot(a, b[...], preferred_element_type=F32) for b in b_refs]

    def finish(vals):
        o_ref[...] = epilogue(vals, [e[...] for e in e_refs]).astype(o_ref.dtype)

    if nk == 1:
        finish(parts)
        return
    k = pl.program_id(2)

    @pl.when(k == 0)
    def _():
        for acc, p in zip(acc_refs, parts):
            acc[...] = p

    @pl.when(jnp.logical_and(k > 0, k < nk - 1))
    def _():
        for acc, p in zip(acc_refs, parts):
            acc[...] += p

    @pl.when(k == nk - 1)
    def _():
        finish([acc[...] + p for acc, p in zip(acc_refs, parts)])


def _matmul(a, bs, extras, epilogue, out_dtype, *, tm, tn, tk, name):
    m, kdim = a.shape
    n = bs[0].shape[1]
    tm, tn, tk = _pick(m, tm), _pick(n, tn), _pick(kdim, tk)
    nk = kdim // tk
    in_specs = [pl.BlockSpec((tm, tk), lambda i, j, k: (i, k))]
    in_specs += [pl.BlockSpec((tk, tn), lambda i, j, k: (k, j)) for _ in bs]
    for _, kind in extras:
        if kind == "mn":
            in_specs.append(pl.BlockSpec((tm, tn), lambda i, j, k: (i, j)))
        else:
            in_specs.append(pl.BlockSpec((1, tn), lambda i, j, k: (0, j)))
    scratch = [pltpu.VMEM((tm, tn), F32) for _ in bs] if nk > 1 else []
    return pl.pallas_call(
        functools.partial(_mm_body, n_b=len(bs), n_e=len(extras), nk=nk, epilogue=epilogue),
        grid=(m // tm, n // tn, nk),
        in_specs=in_specs,
        out_specs=pl.BlockSpec((tm, tn), lambda i, j, k: (i, j)),
        out_shape=jax.ShapeDtypeStruct((m, n), out_dtype),
        scratch_shapes=scratch,
        compiler_params=_params("parallel", "parallel", "arbitrary"),
        name=name,
    )(a, *bs, *[e for e, _ in extras])


def _ep_identity(parts, extras):
    return parts[0]


def _ep_colscale(parts, extras):
    return parts[0] * extras[0]


def _ep_swiglu(parts, extras):
    gate, up = parts
    return gate * (1.0 / (1.0 + jnp.exp(-gate))) * up


def _ep_residual(weight, parts, extras):
    return extras[0] + weight * parts[0]


def _ep_glu(parts, extras):
    z = extras[0].astype(F32)
    return z * (1.0 / (1.0 + jnp.exp(-parts[0])))


def _gelu_tanh(x):
    c = math.sqrt(2.0 / math.pi)
    return x * (0.5 * (1.0 + jnp.tanh(c * (x + 0.044715 * (x * x * x)))))


def _ffn(x, h, w_gate, w_up, w_down):
    act = _matmul(h, [w_gate.astype(BF16), w_up.astype(BF16)], [], _ep_swiglu, BF16,
                  tm=1024, tn=512, tk=4096, name="ffn_up")
    return _matmul(act, [w_down.astype(BF16)], [(x, "mn")],
                   functools.partial(_ep_residual, FFN_RES_WEIGHT), F32,
                   tm=1024, tn=512, tk=5632, name="ffn_down")


def _proj_residual(x, a, w, name):
    return _matmul(a, [w.astype(BF16)], [(x, "mn")], functools.partial(_ep_residual, 1.0), F32,
                   tm=1024, tn=1024, tk=4096, name=name)


def _s5_operators(lam_re, lam_im, log_step, b_re, b_im, c_re, c_im, d_skip):
    g, p = lam_re.shape
    n, t = S5_GROUP, S5_CHUNK
    lam_re, lam_im = lam_re.astype(F32), lam_im.astype(F32)
    step = jnp.exp(log_step.astype(F32))[:, None]
    er, ei = lam_re * step, lam_im * step

    def a_pow(k):
        k = jnp.asarray(k, F32)[None, :, None]
        mag = jnp.exp(er[:, None, :] * k)
        return mag * jnp.cos(ei[:, None, :] * k), mag * jnp.sin(ei[:, None, :] * k)

    def cmul(xr, xi, yr, yi):
        return xr * yr - xi * yi, xr * yi + xi * yr

    a_re, a_im = a_pow([1.0])
    xr, xi = a_re[:, 0] - 1.0, a_im[:, 0]
    den = lam_re * lam_re + lam_im * lam_im
    qr, qi = (xr * lam_re + xi * lam_im) / den, (xi * lam_re - xr * lam_im) / den
    bt_re, bt_im = b_re.astype(F32).transpose(0, 2, 1), b_im.astype(F32).transpose(0, 2, 1)
    bb_re, bb_im = cmul(qr[:, None, :], qi[:, None, :], bt_re, bt_im)
    cc_re, cc_im = c_re.astype(F32), c_im.astype(F32)
    steps = jnp.arange(t, dtype=F32)
    half = t // 2

    def outer(pw, mat):
        r, i = cmul(pw[0][:, :, None, :], pw[1][:, :, None, :], mat[0][:, None], mat[1][:, None])
        return r.reshape(g, t * n, p), i.reshape(g, t * n, p)

    fb_re, fb_im = outer(a_pow(half - steps), (bb_re, bb_im))
    fc_re, fc_im = outer(a_pow(steps - half), (cc_re, cc_im))
    fb = jnp.concatenate([fb_re, -fb_im], axis=-1)
    fc = jnp.concatenate([fc_re, fc_im], axis=-1).transpose(0, 2, 1)
    v_re, v_im = outer(a_pow(t - 1.0 - steps), (bb_re, bb_im))
    v_mat = jnp.concatenate([v_re, v_im], axis=-1).astype(BF16)
    v_swap = jnp.concatenate([v_im, v_re], axis=-1).astype(BF16)
    w_re, w_im = outer(a_pow(steps + 1.0), (cc_re, cc_im))
    w_mat = jnp.concatenate([w_re, -w_im], axis=-1).transpose(0, 2, 1).astype(BF16)
    at_re, at_im = a_pow([float(t)])
    a1 = jnp.concatenate([at_re[:, 0], at_re[:, 0]], axis=-1)
    a2 = jnp.concatenate([-at_im[:, 0], at_im[:, 0]], axis=-1)
    d_row = jnp.tile(d_skip.astype(F32).reshape(g, n), (1, t))
    return fb, fc, v_mat, v_swap, w_mat, d_row, a1, a2


def _s5_core_body(u_ref, fb_ref, fc_ref, v_ref, vs_ref, w_ref, d_ref, a1_ref, a2_ref, z_ref,
                  perm_ref, m_ref, x_ref, loc_ref, locs_ref, prev_ref, s_ref, ss_ref, *, gb, cb, t, n):
    col, blk = pl.program_id(0), pl.program_id(1)
    width = t * n
    lanes = gb * n
    wide = t * lanes
    ln, lw = n.bit_length() - 1, width.bit_length() - 1

    @pl.when(jnp.logical_and(col == 0, blk == 0))
    def _():
        strip = 256
        c = lax.broadcasted_iota(jnp.int32, (strip, wide), 1)
        src = (((c >> ln) & (t - 1)) * lanes) + ((c >> lw) * n) + (c & (n - 1))
        for r0 in range(0, wide, strip):
            r = lax.broadcasted_iota(jnp.int32, (strip, wide), 0) + r0
            perm_ref[r0:r0 + strip, :] = jnp.where(r == src, 1.0, 0.0).astype(BF16)

    @pl.when(blk == 0)
    def _():
        rj = lax.broadcasted_iota(jnp.int32, (width, width), 0)
        ct = lax.broadcasted_iota(jnp.int32, (width, width), 1)
        for g in range(gb):
            fbh, fbl = _split2(fb_ref[g])
            fch, fcl = _split2(fc_ref[g])
            mm = (jnp.dot(fbh, fch, preferred_element_type=F32) + jnp.dot(fbh, fcl, preferred_element_type=F32)
                  + jnp.dot(fbl, fch, preferred_element_type=F32))
            mm = jnp.where((ct >> ln) >= (rj >> ln), mm, 0.0) + jnp.where(rj == ct, d_ref[pl.ds(g, 1), :], 0.0)
            m_ref[g] = mm.astype(BF16)
        s_ref[...] = jnp.zeros_like(s_ref)
        ss_ref[...] = jnp.zeros_like(ss_ref)

    x_ref[...] = u_ref[...].astype(F32)
    xcat = jnp.concatenate([x_ref[pl.ds(tt, cb, stride=t), :] for tt in range(t)], axis=1).astype(BF16)
    ug_all = jnp.dot(xcat, perm_ref[...], preferred_element_type=F32).astype(BF16)
    for g in range(gb):
        ug = ug_all[:, g * width:(g + 1) * width]
        loc_ref[g * cb:(g + 1) * cb, :] = jnp.dot(ug, v_ref[g], preferred_element_type=F32)
        locs_ref[g * cb:(g + 1) * cb, :] = jnp.dot(ug, vs_ref[g], preferred_element_type=F32)
    a1, a2 = a1_ref[...], a2_ref[...]

    def step(c, carry):
        s, ss = carry
        rows = pl.ds(c, gb, stride=cb)
        prev_ref[rows, :] = s
        s_new = a1 * s + a2 * ss + loc_ref[rows, :]
        ss_new = a1 * ss - a2 * s + locs_ref[rows, :]
        return s_new, ss_new

    s_ref[...], ss_ref[...] = lax.fori_loop(0, cb, step, (s_ref[...], ss_ref[...]))
    zs = []
    for g in range(gb):
        ug = ug_all[:, g * width:(g + 1) * width]
        y = jnp.dot(ug, m_ref[g], preferred_element_type=F32)
        y += jnp.dot(prev_ref[g * cb:(g + 1) * cb, :].astype(BF16), w_ref[g], preferred_element_type=F32)
        zs.append(_gelu_tanh(y).astype(BF16))
    zcat = lax.dot_general(jnp.concatenate(zs, axis=1), perm_ref[...], (((1,), (1,)), ((), ())),
                           preferred_element_type=F32)
    for tt in range(t):
        x_ref[pl.ds(tt, cb, stride=t), :] = zcat[:, tt * lanes:(tt + 1) * lanes]
    z_ref[...] = x_ref[...].astype(z_ref.dtype)


def _s5_core(u, ops):
    fb, fc, v_mat, v_swap, w_mat, d_row, a1, a2 = ops
    seq, d = u.shape
    t, n = S5_CHUNK, S5_GROUP
    gb = V7X_LANES // n
    width, state = t * n, a1.shape[-1]
    chunks = seq // t
    cb = _pick(chunks, 512, V7X_SUBLANES)
    wide = t * V7X_LANES
    blk3 = lambda d1, d2: pl.BlockSpec((gb, d1, d2), lambda j, i: (j, 0, 0))
    blk2 = lambda d1: pl.BlockSpec((gb, d1), lambda j, i: (j, 0))
    return pl.pallas_call(
        functools.partial(_s5_core_body, gb=gb, cb=cb, t=t, n=n),
        grid=(d // V7X_LANES, chunks // cb),
        in_specs=[pl.BlockSpec((cb * t, V7X_LANES), lambda j, i: (i, j)),
                  blk3(width, state), blk3(state, width), blk3(width, state), blk3(width, state),
                  blk3(state, width), blk2(width), blk2(state), blk2(state)],
        out_specs=pl.BlockSpec((cb * t, V7X_LANES), lambda j, i: (i, j)),
        out_shape=jax.ShapeDtypeStruct((seq, d), BF16),
        scratch_shapes=[pltpu.VMEM((wide, wide), BF16),
                        pltpu.VMEM((gb, width, width), BF16),
                        pltpu.VMEM((cb * t, V7X_LANES), F32),
                        pltpu.VMEM((gb * cb, state), F32),
                        pltpu.VMEM((gb * cb, state), F32),
                        pltpu.VMEM((gb * cb, state), F32),
                        pltpu.VMEM((gb, state), F32),
                        pltpu.VMEM((gb, state), F32)],
        compiler_params=_params("arbitrary", "arbitrary"),
        name="s5_core",
    )(u, fb, fc, v_mat, v_swap, w_mat, d_row, a1, a2)


def _s5_mixer(x, h, w_in, lam_re, lam_im, log_step, b_re, b_im, c_re, c_im, d_skip, w_glu, w_out):
    ops = _s5_operators(lam_re, lam_im, log_step, b_re, b_im, c_re, c_im, d_skip)
    u = _matmul(h, [w_in.astype(BF16)], [], _ep_identity, BF16, tm=1024, tn=1024, tk=4096, name="s5_in")
    z = _s5_core(u, ops)
    gated = _matmul(z, [w_glu.astype(BF16)], [(z, "mn")], _ep_glu, BF16,
                    tm=1024, tn=1024, tk=4096, name="s5_glu")
    return _proj_residual(x, gated, w_out, "s5_out")


def _pool_band_body(u_ref, o_ref, hist_ref, *, tm, tn, hist, group_cols):
    j, i = pl.program_id(0), pl.program_id(1)

    @pl.when(i == 0)
    def _():
        hist_ref[...] = jnp.zeros_like(hist_ref)

    u = u_ref[...]
    ub = u.astype(BF16)
    ext = jnp.concatenate([hist_ref[...], ub], axis=0)
    gid = (j * tn) // group_cols
    win = jnp.int32(POOL_WINDOWS[0])
    for k, w in enumerate(POOL_WINDOWS[1:], start=1):
        win = jnp.where(gid == k, jnp.int32(w), win)
    lag = (lax.broadcasted_iota(jnp.int32, (tm, hist + tm), 0) + hist
           - lax.broadcasted_iota(jnp.int32, (tm, hist + tm), 1))
    band = jnp.where(lag >= 0, jnp.where(lag < win, 1.0, 0.0), 0.0).astype(BF16)
    wsum = jnp.dot(band, ext, preferred_element_type=F32)
    count = jnp.minimum(i * tm + lax.broadcasted_iota(jnp.int32, (tm, 1), 0) + 1, win).astype(F32)
    o_ref[...] = (wsum / count - u).astype(o_ref.dtype)
    hist_ref[...] = ub[tm - hist:, :]


def _pool_band(u, n_groups):
    seq, d = u.shape
    group_cols = d // n_groups
    tm = _pick(seq, 512)
    tn = _pick(group_cols, 512)
    hist = min(V7X_LANES, tm)
    assert hist >= max(POOL_WINDOWS)
    return pl.pallas_call(
        functools.partial(_pool_band_body, tm=tm, tn=tn, hist=hist, group_cols=group_cols),
        grid=(d // tn, seq // tm),
        in_specs=[pl.BlockSpec((tm, tn), lambda j, i: (i, j))],
        out_specs=pl.BlockSpec((tm, tn), lambda j, i: (i, j)),
        out_shape=jax.ShapeDtypeStruct((seq, d), BF16),
        scratch_shapes=[pltpu.VMEM((hist, tn), BF16)],
        compiler_params=_params("parallel", "arbitrary"),
        name="pool_band",
    )(u)


def _pool_group_body(a_ref, w_ref, s_ref, o_ref):
    o_ref[...] = (jnp.dot(a_ref[...], w_ref[...], preferred_element_type=F32) * s_ref[...]).astype(o_ref.dtype)


def _pool_group(diff, w_group, scale):
    seq, d = diff.shape
    n_groups, gc, _ = w_group.shape
    tm = _pick(seq, 1024)
    return pl.pallas_call(
        _pool_group_body,
        grid=(n_groups, seq // tm),
        in_specs=[pl.BlockSpec((tm, gc), lambda g, i: (i, g)),
                  pl.BlockSpec((None, gc, gc), lambda g, i: (g, 0, 0)),
                  pl.BlockSpec((1, gc), lambda g, i: (0, g))],
        out_specs=pl.BlockSpec((tm, gc), lambda g, i: (i, g)),
        out_shape=jax.ShapeDtypeStruct((seq, d), BF16),
        compiler_params=_params("parallel", "parallel"),
        name="pool_group",
    )(diff, w_group.astype(BF16), scale.reshape(1, d).astype(F32))


def _pool_mixer(x, h, w_in, w_group, scale, w_out):
    u = _matmul(h, [w_in.astype(BF16)], [], _ep_identity, F32, tm=1024, tn=1024, tk=4096, name="pool_in")
    diff = _pool_band(u, len(POOL_WINDOWS))
    mixed = _pool_group(diff, w_group, scale)
    return _proj_residual(x, mixed, w_out, "pool_out")


def _forget_cumsum_body(fl_ref, b_ref, cum_ref, cumt_ref, carry_ref, *, tm):
    @pl.when(pl.program_id(0) == 0)
    def _():
        carry_ref[...] = jnp.zeros_like(carry_ref)

    x = fl_ref[...] + b_ref[...]
    logf = -(jnp.maximum(-x, 0.0) + jnp.log1p(jnp.exp(-jnp.abs(x))))
    tri = jnp.where(lax.broadcasted_iota(jnp.int32, (tm, tm), 0)
                    >= lax.broadcasted_iota(jnp.int32, (tm, tm), 1), 1.0, 0.0).astype(BF16)
    cs = carry_ref[...]
    for part in _split3(logf):
        cs = cs + jnp.dot(tri, part, preferred_element_type=F32)
    cum_ref[...] = cs * LOG2_E
    cumt_ref[...] = (cs * LOG2_E).T
    carry_ref[...] = cs[tm - 1:tm, :]


def _forget_cumsum(fl, b_row):
    seq, hp = fl.shape
    tm = _pick(seq, 256)
    return pl.pallas_call(
        functools.partial(_forget_cumsum_body, tm=tm),
        grid=(seq // tm,),
        in_specs=[pl.BlockSpec((tm, hp), lambda i: (i, 0)),
                  pl.BlockSpec((1, hp), lambda i: (0, 0))],
        out_specs=[pl.BlockSpec((tm, hp), lambda i: (i, 0)),
                   pl.BlockSpec((hp, tm), lambda i: (0, i))],
        out_shape=[jax.ShapeDtypeStruct((seq, hp), F32), jax.ShapeDtypeStruct((hp, seq), F32)],
        scratch_shapes=[pltpu.VMEM((1, hp), F32)],
        compiler_params=_params("arbitrary"),
        name="forget_cumsum",
    )(fl, b_row)


def _fox_attn_body(q_ref, k_ref, v_ref, cum_ref, cumt_ref, o_ref, s_ref, cq_ref, m_ref, l_ref, acc_ref,
                   *, tq, tk, rs):
    head, qi = pl.program_id(0), pl.program_id(1)
    lw = V7X_LANES
    q = q_ref[...]
    lane = lax.broadcasted_iota(jnp.int32, cum_ref.shape, 1)
    cq = jnp.sum(jnp.where(lane == head, cum_ref[...], 0.0), axis=-1, keepdims=True)
    cq_ref[...] = jnp.broadcast_to(cq, (tq, lw))
    ck_row = head % V7X_SUBLANES
    m_ref[...] = jnp.full_like(m_ref, -jnp.inf)
    l_ref[...] = jnp.zeros_like(l_ref)
    acc_ref[...] = jnp.zeros_like(acc_ref)

    def scores(kj, slot):
        start = pl.multiple_of(kj * tk, tk)
        ck = cumt_ref[pl.ds(ck_row, 1), pl.ds(start, tk)]
        s_ref[slot] = lax.dot_general(q, k_ref[pl.ds(start, tk), :], (((1,), (1,)), ((), ())),
                                      preferred_element_type=F32) - ck

    def update(kj, slot, diagonal):
        start = pl.multiple_of(kj * tk, tk)
        vb = v_ref[pl.ds(start, tk), :]
        for r0 in range(0, tq, rs):
            rows = pl.ds(r0, rs)
            s = s_ref[slot, rows, :]
            if diagonal:
                qpos = qi * tq + r0 + lax.broadcasted_iota(jnp.int32, (rs, tk), 0)
                kpos = start + lax.broadcasted_iota(jnp.int32, (rs, tk), 1)
                s = jnp.where(qpos >= kpos, s, -jnp.inf)
            cq_rows = cq_ref[rows, :]
            m_prev = m_ref[rows, :]
            m_new = jnp.maximum(m_prev, jnp.max(s, axis=-1, keepdims=True) + cq_rows)
            alpha = jnp.exp2(m_prev - m_new)
            shift = m_new - cq_rows
            ps = [jnp.exp2(s[:, c * lw:(c + 1) * lw] - shift) for c in range(tk // lw)]
            l_ref[rows, :] = alpha * l_ref[rows, :] + functools.reduce(lambda a, b: a + b, ps)
            p = jnp.concatenate([x.astype(BF16) for x in ps], axis=1)
            acc_ref[rows, :] = alpha * acc_ref[rows, :] + jnp.dot(p, vb, preferred_element_type=F32)
            m_ref[rows, :] = m_new

    first_diag = qi * (tq // tk)
    scores(0, 0)

    def pair(kk, carry):
        b = 2 * kk
        scores(b + 1, 1)
        update(b, 0, False)
        scores(b + 2, 0)
        update(b + 1, 1, False)
        return carry

    lax.fori_loop(0, first_diag // 2, pair, 0)
    scores(first_diag + 1, 1)
    update(first_diag, 0, True)
    update(first_diag + 1, 1, True)
    o_ref[...] = (acc_ref[...] / jnp.sum(l_ref[...], axis=-1, keepdims=True)).astype(o_ref.dtype)


def _fox_attention(qkv, cum, cumt, n_heads, tq):
    seq = qkv.shape[0]
    dh = FOX_HEAD_DIM
    hp = cum.shape[1]
    tk = tq // 2
    rs = _pick(tq, 256, V7X_SUBLANES)
    return pl.pallas_call(
        functools.partial(_fox_attn_body, tq=tq, tk=tk, rs=rs),
        grid=(n_heads, seq // tq),
        in_specs=[pl.BlockSpec((tq, dh), lambda h, i: (i, h)),
                  pl.BlockSpec((seq, dh), lambda h, i: (0, n_heads + h)),
                  pl.BlockSpec((seq, dh), lambda h, i: (0, 2 * n_heads + h)),
                  pl.BlockSpec((tq, hp), lambda h, i: (i, 0)),
                  pl.BlockSpec((V7X_SUBLANES, seq), lambda h, i: (h // V7X_SUBLANES, 0))],
        out_specs=pl.BlockSpec((tq, dh), lambda h, i: (i, h)),
        out_shape=jax.ShapeDtypeStruct((seq, n_heads * dh), BF16),
        scratch_shapes=[pltpu.VMEM((2, tq, tk), F32), pltpu.VMEM((tq, V7X_LANES), F32),
                        pltpu.VMEM((tq, V7X_LANES), F32), pltpu.VMEM((tq, V7X_LANES), F32),
                        pltpu.VMEM((tq, dh), F32)],
        compiler_params=_params("parallel", "arbitrary"),
        name="fox_attention",
    )(qkv, qkv, qkv, cum, cumt)


def _fox_mixer(x, h, w_in, b_f, w_out, tq=1024):
    seq, d = h.shape
    n_heads = d // FOX_HEAD_DIM
    hp = -(-n_heads // V7X_LANES) * V7X_LANES
    w_qkv = w_in[:, :3 * d].astype(BF16)
    w_f = jnp.pad(w_in[:, 3 * d:], ((0, 0), (0, hp - n_heads))).astype(BF16)
    b_row = jnp.pad(b_f.astype(F32), (0, hp - n_heads)).reshape(1, hp)
    sm_scale = FOX_HEAD_DIM ** -0.5 * LOG2_E
    col_scale = jnp.concatenate([jnp.full((1, d), sm_scale, F32), jnp.ones((1, 2 * d), F32)], axis=1)
    qkv = _matmul(h, [w_qkv], [(col_scale, "n")], _ep_colscale, BF16,
                  tm=1024, tn=1024, tk=4096, name="fox_qkv")
    fl = _matmul(h, [w_f], [], _ep_identity, F32, tm=1024, tn=V7X_LANES, tk=4096, name="fox_gate")
    cum, cumt = _forget_cumsum(fl, b_row)
    o = _fox_attention(qkv, cum, cumt, n_heads, _pick(seq, tq))
    return _proj_residual(x, o, w_out, "fox_out")


def _trunk(x, p):
    depth = p["norm_ffn1"].shape[0]
    for i in range(depth):
        h = _rmsnorm(x, p["norm_ffn1"][i], BF16)
        x = _ffn(x, h, p["ffn1_w_gate"][i], p["ffn1_w_up"][i], p["ffn1_w_down"][i])
        h = _rmsnorm(x, p["norm_mix"][i], BF16)
        j, kind = divmod(i, N_MIXERS)
        if kind == 0:
            x = _s5_mixer(x, h, p["s5_w_in"][j], p["s5_lam_re"][j], p["s5_lam_im"][j], p["s5_log_step"][j],
                          p["s5_b_re"][j], p["s5_b_im"][j], p["s5_c_re"][j], p["s5_c_im"][j],
                          p["s5_d"][j], p["s5_w_glu"][j], p["s5_w_out"][j])
        elif kind == 1:
            x = _pool_mixer(x, h, p["pool_w_in"][j], p["pool_w_group"][j], p["pool_scale"][j], p["pool_w_out"][j])
        else:
            x = _fox_mixer(x, h, p["fox_w_in"][j], p["fox_b_f"][j], p["fox_w_out"][j])
        h = _rmsnorm(x, p["norm_ffn2"][i], BF16)
        x = _ffn(x, h, p["ffn2_w_gate"][i], p["ffn2_w_up"][i], p["ffn2_w_down"][i])
    return _rmsnorm(x, p["norm_final"], F32)


def kernel(x, norm_ffn1, ffn1_w_gate, ffn1_w_up, ffn1_w_down, norm_mix, s5_w_in, s5_lam_re, s5_lam_im, s5_log_step, s5_b_re, s5_b_im, s5_c_re, s5_c_im, s5_d, s5_w_glu, s5_w_out, pool_w_in, pool_w_group, pool_scale, pool_w_out, fox_w_in, fox_b_f, fox_w_out, norm_ffn2, ffn2_w_gate, ffn2_w_up, ffn2_w_down, norm_final):
    p = dict(norm_ffn1=norm_ffn1, ffn1_w_gate=ffn1_w_gate, ffn1_w_up=ffn1_w_up, ffn1_w_down=ffn1_w_down,
             norm_mix=norm_mix, s5_w_in=s5_w_in, s5_lam_re=s5_lam_re, s5_lam_im=s5_lam_im,
             s5_log_step=s5_log_step, s5_b_re=s5_b_re, s5_b_im=s5_b_im, s5_c_re=s5_c_re, s5_c_im=s5_c_im,
             s5_d=s5_d, s5_w_glu=s5_w_glu, s5_w_out=s5_w_out, pool_w_in=pool_w_in, pool_w_group=pool_w_group,
             pool_scale=pool_scale, pool_w_out=pool_w_out, fox_w_in=fox_w_in, fox_b_f=fox_b_f,
             fox_w_out=fox_w_out, norm_ffn2=norm_ffn2, ffn2_w_gate=ffn2_w_gate, ffn2_w_up=ffn2_w_up,
             ffn2_w_down=ffn2_w_down, norm_final=norm_final)
    return jnp.stack([_trunk(x[b], p) for b in range(x.shape[0])])
```

```python
import functools
import math

import jax
import jax.numpy as jnp
from jax import lax
from jax.experimental import pallas as pl
from jax.experimental.pallas import tpu as pltpu

RMS_EPS = 1e-6
FFN_RES_WEIGHT = 0.5
N_MIXERS = 3
S5_GROUP = 16
S5_CHUNK = 16
POOL_WINDOWS = (2, 4, 8, 16)
FOX_HEAD_DIM = 128
LOG2_E = 1.0 / math.log(2.0)

V7X_LANES = 128
V7X_SUBLANES = 8
V7X_VMEM_BYTES = 64 * 2**20
VMEM_LIMIT = V7X_VMEM_BYTES - 8 * 2**20

F32 = jnp.float32
BF16 = jnp.bfloat16


def _pick(dim, pref, align=V7X_LANES):
    if dim <= pref:
        return dim
    t = (pref // align) * align
    while t >= align:
        if dim % t == 0:
            return t
        t -= align
    return dim


def _params(*semantics):
    return pltpu.CompilerParams(dimension_semantics=semantics, vmem_limit_bytes=VMEM_LIMIT)


def _split2(x):
    hi = x.astype(BF16)
    return hi, (x - hi.astype(F32)).astype(BF16)


def _split3(x):
    hi = x.astype(BF16)
    r1 = x - hi.astype(F32)
    mid = r1.astype(BF16)
    return hi, mid, (r1 - mid.astype(F32)).astype(BF16)


def _rmsnorm_body(x_ref, g_ref, o_ref):
    x = x_ref[...]
    ms = jnp.mean(x * x, axis=-1, keepdims=True)
    o_ref[...] = (x * lax.rsqrt(ms + RMS_EPS) * g_ref[...]).astype(o_ref.dtype)


def _rmsnorm(x, g, out_dtype):
    rows, d = x.shape
    tr = _pick(rows, 256)
    return pl.pallas_call(
        _rmsnorm_body,
        grid=(rows // tr,),
        in_specs=[pl.BlockSpec((tr, d), lambda i: (i, 0)),
                  pl.BlockSpec((1, d), lambda i: (0, 0))],
        out_specs=pl.BlockSpec((tr, d), lambda i: (i, 0)),
        out_shape=jax.ShapeDtypeStruct((rows, d), out_dtype),
        compiler_params=_params("parallel"),
        name="rmsnorm",
    )(x, g.reshape(1, d).astype(F32))


def _mm_body(*refs, n_b, n_e, epilogue):
    a_ref = refs[0]
    b_refs = refs[1:1 + n_b]
    e_refs = refs[1 + n_b:1 + n_b + n_e]
    o_ref = refs[1 + n_b + n_e]
    w_refs = refs[2 + n_b + n_e:]

    @pl.when(pl.program_id(1) == 0)
    def _():
        for w, b in zip(w_refs, b_refs):
            w[...] = b[...].astype(BF16)

    a = a_ref[...]
    parts = [jnp.dot(a, w[...], preferred_element_type=F32) for w in w_refs]
    o_ref[...] = epilogue(parts, [e[...] for e in e_refs]).astype(o_ref.dtype)


def _matmul(a, bs, extras, epilogue, out_dtype, *, tm, tn, name, n=None):
    m, kdim = a.shape
    n = bs[0].shape[1] if n is None else n
    tm, tn = _pick(m, tm), _pick(n, tn)
    in_specs = [pl.BlockSpec((tm, kdim), lambda j, i: (i, 0))]
    in_specs += [pl.BlockSpec((kdim, tn), lambda j, i: (0, j), pipeline_mode=pl.Buffered(1)) for _ in bs]
    for _, kind in extras:
        if kind == "mn":
            in_specs.append(pl.BlockSpec((tm, tn), lambda j, i: (i, j)))
        else:
            in_specs.append(pl.BlockSpec((1, tn), lambda j, i: (0, j)))
    return pl.pallas_call(
        functools.partial(_mm_body, n_b=len(bs), n_e=len(extras), epilogue=epilogue),
        grid=(n // tn, m // tm),
        in_specs=in_specs,
        out_specs=pl.BlockSpec((tm, tn), lambda j, i: (i, j)),
        out_shape=jax.ShapeDtypeStruct((m, n), out_dtype),
        scratch_shapes=[pltpu.VMEM((kdim, tn), BF16) for _ in bs],
        compiler_params=_params("arbitrary", "arbitrary"),
        name=name,
    )(a, *[b.astype(F32) for b in bs], *[e for e, _ in extras])


def _ep_identity(parts, extras):
    return parts[0]


def _ep_colscale(parts, extras):
    return parts[0] * extras[0]


def _ep_swiglu(parts, extras):
    gate, up = parts
    return gate * (1.0 / (1.0 + jnp.exp(-gate))) * up


def _ep_residual(weight, parts, extras):
    return extras[0] + weight * parts[0]


def _ep_glu(parts, extras):
    z = extras[0].astype(F32)
    return z * (1.0 / (1.0 + jnp.exp(-parts[0])))


def _gelu_tanh(x):
    c = math.sqrt(2.0 / math.pi)
    return x * (0.5 * (1.0 + jnp.tanh(c * (x + 0.044715 * (x * x * x)))))


def _ffn(x, h, w_gate, w_up, w_down):
    act = _matmul(h, [w_gate, w_up], [], _ep_swiglu, BF16, tm=1024, tn=512, name="ffn_up")
    return _matmul(act, [w_down], [(x, "mn")], functools.partial(_ep_residual, FFN_RES_WEIGHT), F32,
                   tm=512, tn=512, name="ffn_down")


def _proj_residual(x, a, w, name):
    return _matmul(a, [w], [(x, "mn")], functools.partial(_ep_residual, 1.0), F32,
                   tm=1024, tn=512, name=name)


def _s5_operators(lam_re, lam_im, log_step, b_re, b_im, c_re, c_im, d_skip):
    g, p = lam_re.shape
    n, t = S5_GROUP, S5_CHUNK
    lam_re, lam_im = lam_re.astype(F32), lam_im.astype(F32)
    step = jnp.exp(log_step.astype(F32))[:, None]
    er, ei = lam_re * step, lam_im * step

    def a_pow(k):
        k = jnp.asarray(k, F32)[None, :, None]
        mag = jnp.exp(er[:, None, :] * k)
        return mag * jnp.cos(ei[:, None, :] * k), mag * jnp.sin(ei[:, None, :] * k)

    def cmul(xr, xi, yr, yi):
        return xr * yr - xi * yi, xr * yi + xi * yr

    a_re, a_im = a_pow([1.0])
    xr, xi = a_re[:, 0] - 1.0, a_im[:, 0]
    den = lam_re * lam_re + lam_im * lam_im
    qr, qi = (xr * lam_re + xi * lam_im) / den, (xi * lam_re - xr * lam_im) / den
    bt_re, bt_im = b_re.astype(F32).transpose(0, 2, 1), b_im.astype(F32).transpose(0, 2, 1)
    bb_re, bb_im = cmul(qr[:, None, :], qi[:, None, :], bt_re, bt_im)
    cc_re, cc_im = c_re.astype(F32), c_im.astype(F32)
    steps = jnp.arange(t, dtype=F32)
    half = t // 2

    def outer(pw, mat):
        r, i = cmul(pw[0][:, :, None, :], pw[1][:, :, None, :], mat[0][:, None], mat[1][:, None])
        return r.reshape(g, t * n, p), i.reshape(g, t * n, p)

    fb_re, fb_im = outer(a_pow(half - steps), (bb_re, bb_im))
    fc_re, fc_im = outer(a_pow(steps - half), (cc_re, cc_im))
    fb = jnp.concatenate([fb_re, -fb_im], axis=-1)
    fc = jnp.concatenate([fc_re, fc_im], axis=-1).transpose(0, 2, 1)
    v_re, v_im = outer(a_pow(t - 1.0 - steps), (bb_re, bb_im))
    v_mat = jnp.concatenate([v_re, v_im], axis=-1).astype(BF16)
    v_swap = jnp.concatenate([v_im, v_re], axis=-1).astype(BF16)
    w_re, w_im = outer(a_pow(steps + 1.0), (cc_re, cc_im))
    w_mat = jnp.concatenate([w_re, -w_im], axis=-1).transpose(0, 2, 1).astype(BF16)
    at_re, at_im = a_pow([float(t)])
    a1 = jnp.concatenate([at_re[:, 0], at_re[:, 0]], axis=-1)
    a2 = jnp.concatenate([-at_im[:, 0], at_im[:, 0]], axis=-1)
    d_row = jnp.tile(d_skip.astype(F32).reshape(g, n), (1, t))
    return fb, fc, v_mat, v_swap, w_mat, d_row, a1, a2


def _s5_core_body(u_ref, fb_ref, fc_ref, v_ref, vs_ref, w_ref, d_ref, a1_ref, a2_ref, z_ref,
                  perm_ref, m_ref, x_ref, loc_ref, locs_ref, prev_ref, s_ref, ss_ref, *, gb, cb, t, n):
    col, blk = pl.program_id(0), pl.program_id(1)
    width = t * n
    lanes = gb * n
    wide = t * lanes
    ln, lw = n.bit_length() - 1, width.bit_length() - 1

    @pl.when(jnp.logical_and(col == 0, blk == 0))
    def _():
        strip = 256
        c = lax.broadcasted_iota(jnp.int32, (strip, wide), 1)
        src = (((c >> ln) & (t - 1)) * lanes) + ((c >> lw) * n) + (c & (n - 1))
        for r0 in range(0, wide, strip):
            r = lax.broadcasted_iota(jnp.int32, (strip, wide), 0) + r0
            perm_ref[r0:r0 + strip, :] = jnp.where(r == src, 1.0, 0.0).astype(BF16)

    @pl.when(blk == 0)
    def _():
        rj = lax.broadcasted_iota(jnp.int32, (width, width), 0)
        ct = lax.broadcasted_iota(jnp.int32, (width, width), 1)
        for g in range(gb):
            fbh, fbl = _split2(fb_ref[g])
            fch, fcl = _split2(fc_ref[g])
            mm = (jnp.dot(fbh, fch, preferred_element_type=F32) + jnp.dot(fbh, fcl, preferred_element_type=F32)
                  + jnp.dot(fbl, fch, preferred_element_type=F32))
            mm = jnp.where((ct >> ln) >= (rj >> ln), mm, 0.0) + jnp.where(rj == ct, d_ref[pl.ds(g, 1), :], 0.0)
            m_ref[g] = mm.astype(BF16)
        s_ref[...] = jnp.zeros_like(s_ref)
        ss_ref[...] = jnp.zeros_like(ss_ref)

    x_ref[...] = u_ref[...].astype(F32)
    xcat = jnp.concatenate([x_ref[pl.ds(tt, cb, stride=t), :] for tt in range(t)], axis=1).astype(BF16)
    ug_all = jnp.dot(xcat, perm_ref[...], preferred_element_type=F32).astype(BF16)
    for g in range(gb):
        ug = ug_all[:, g * width:(g + 1) * width]
        loc_ref[g * cb:(g + 1) * cb, :] = jnp.dot(ug, v_ref[g], preferred_element_type=F32)
        locs_ref[g * cb:(g + 1) * cb, :] = jnp.dot(ug, vs_ref[g], preferred_element_type=F32)
    a1, a2 = a1_ref[...], a2_ref[...]

    def step(c, carry):
        s, ss = carry
        rows = pl.ds(c, gb, stride=cb)
        prev_ref[rows, :] = s
        s_new = a1 * s + a2 * ss + loc_ref[rows, :]
        ss_new = a1 * ss - a2 * s + locs_ref[rows, :]
        return s_new, ss_new

    s_ref[...], ss_ref[...] = lax.fori_loop(0, cb, step, (s_ref[...], ss_ref[...]), unroll=V7X_SUBLANES)
    zs = []
    for g in range(gb):
        ug = ug_all[:, g * width:(g + 1) * width]
        y = jnp.dot(ug, m_ref[g], preferred_element_type=F32)
        y += jnp.dot(prev_ref[g * cb:(g + 1) * cb, :].astype(BF16), w_ref[g], preferred_element_type=F32)
        zs.append(_gelu_tanh(y).astype(BF16))
    zcat = lax.dot_general(jnp.concatenate(zs, axis=1), perm_ref[...], (((1,), (1,)), ((), ())),
                           preferred_element_type=F32)
    for tt in range(t):
        x_ref[pl.ds(tt, cb, stride=t), :] = zcat[:, tt * lanes:(tt + 1) * lanes]
    z_ref[...] = x_ref[...].astype(z_ref.dtype)


def _s5_core(u, ops):
    fb, fc, v_mat, v_swap, w_mat, d_row, a1, a2 = ops
    seq, d = u.shape
    t, n = S5_CHUNK, S5_GROUP
    gb = V7X_LANES // n
    width, state = t * n, a1.shape[-1]
    chunks = seq // t
    cb = _pick(chunks, 512, V7X_SUBLANES)
    wide = t * V7X_LANES
    blk3 = lambda d1, d2: pl.BlockSpec((gb, d1, d2), lambda j, i: (j, 0, 0))
    blk2 = lambda d1: pl.BlockSpec((gb, d1), lambda j, i: (j, 0))
    return pl.pallas_call(
        functools.partial(_s5_core_body, gb=gb, cb=cb, t=t, n=n),
        grid=(d // V7X_LANES, chunks // cb),
        in_specs=[pl.BlockSpec((cb * t, V7X_LANES), lambda j, i: (i, j)),
                  blk3(width, state), blk3(state, width), blk3(width, state), blk3(width, state),
                  blk3(state, width), blk2(width), blk2(state), blk2(state)],
        out_specs=pl.BlockSpec((cb * t, V7X_LANES), lambda j, i: (i, j)),
        out_shape=jax.ShapeDtypeStruct((seq, d), BF16),
        scratch_shapes=[pltpu.VMEM((wide, wide), BF16),
                        pltpu.VMEM((gb, width, width), BF16),
                        pltpu.VMEM((cb * t, V7X_LANES), F32),
                        pltpu.VMEM((gb * cb, state), F32),
                        pltpu.VMEM((gb * cb, state), F32),
                        pltpu.VMEM((gb * cb, state), F32),
                        pltpu.VMEM((gb, state), F32),
                        pltpu.VMEM((gb, state), F32)],
        compiler_params=_params("arbitrary", "arbitrary"),
        name="s5_core",
    )(u, fb, fc, v_mat, v_swap, w_mat, d_row, a1, a2)


def _s5_mixer(x, h, w_in, lam_re, lam_im, log_step, b_re, b_im, c_re, c_im, d_skip, w_glu, w_out):
    ops = _s5_operators(lam_re, lam_im, log_step, b_re, b_im, c_re, c_im, d_skip)
    u = _matmul(h, [w_in], [], _ep_identity, BF16, tm=1024, tn=512, name="s5_in")
    z = _s5_core(u, ops)
    gated = _matmul(z, [w_glu], [(z, "mn")], _ep_glu, BF16, tm=1024, tn=512, name="s5_glu")
    return _proj_residual(x, gated, w_out, "s5_out")


def _pool_band_body(u_ref, o_ref, hist_ref, *, tm, tn, hist, group_cols):
    j, i = pl.program_id(0), pl.program_id(1)

    @pl.when(i == 0)
    def _():
        hist_ref[...] = jnp.zeros_like(hist_ref)

    u = u_ref[...]
    ub = u.astype(BF16)
    ext = jnp.concatenate([hist_ref[...], ub], axis=0)
    gid = (j * tn) // group_cols
    win = jnp.int32(POOL_WINDOWS[0])
    for k, w in enumerate(POOL_WINDOWS[1:], start=1):
        win = jnp.where(gid == k, jnp.int32(w), win)
    lag = (lax.broadcasted_iota(jnp.int32, (tm, hist + tm), 0) + hist
           - lax.broadcasted_iota(jnp.int32, (tm, hist + tm), 1))
    band = jnp.where(lag >= 0, jnp.where(lag < win, 1.0, 0.0), 0.0).astype(BF16)
    wsum = jnp.dot(band, ext, preferred_element_type=F32)
    count = jnp.minimum(i * tm + lax.broadcasted_iota(jnp.int32, (tm, 1), 0) + 1, win).astype(F32)
    o_ref[...] = (wsum / count - u).astype(o_ref.dtype)
    hist_ref[...] = ub[tm - hist:, :]


def _pool_band(u, n_groups):
    seq, d = u.shape
    group_cols = d // n_groups
    tm = _pick(seq, 512)
    tn = _pick(group_cols, 512)
    hist = min(V7X_LANES, tm)
    assert hist >= max(POOL_WINDOWS)
    return pl.pallas_call(
        functools.partial(_pool_band_body, tm=tm, tn=tn, hist=hist, group_cols=group_cols),
        grid=(d // tn, seq // tm),
        in_specs=[pl.BlockSpec((tm, tn), lambda j, i: (i, j))],
        out_specs=pl.BlockSpec((tm, tn), lambda j, i: (i, j)),
        out_shape=jax.ShapeDtypeStruct((seq, d), BF16),
        scratch_shapes=[pltpu.VMEM((hist, tn), BF16)],
        compiler_params=_params("parallel", "arbitrary"),
        name="pool_band",
    )(u)


def _pool_group_body(a_ref, w_ref, s_ref, o_ref):
    o_ref[...] = (jnp.dot(a_ref[...], w_ref[...], preferred_element_type=F32) * s_ref[...]).astype(o_ref.dtype)


def _pool_group(diff, w_group, scale):
    seq, d = diff.shape
    n_groups, gc, _ = w_group.shape
    tm = _pick(seq, 1024)
    return pl.pallas_call(
        _pool_group_body,
        grid=(n_groups, seq // tm),
        in_specs=[pl.BlockSpec((tm, gc), lambda g, i: (i, g)),
                  pl.BlockSpec((None, gc, gc), lambda g, i: (g, 0, 0)),
                  pl.BlockSpec((1, gc), lambda g, i: (0, g))],
        out_specs=pl.BlockSpec((tm, gc), lambda g, i: (i, g)),
        out_shape=jax.ShapeDtypeStruct((seq, d), BF16),
        compiler_params=_params("parallel", "parallel"),
        name="pool_group",
    )(diff, w_group.astype(BF16), scale.reshape(1, d).astype(F32))


def _pool_mixer(x, h, w_in, w_group, scale, w_out):
    u = _matmul(h, [w_in], [], _ep_identity, F32, tm=1024, tn=512, name="pool_in")
    diff = _pool_band(u, len(POOL_WINDOWS))
    mixed = _pool_group(diff, w_group, scale)
    return _proj_residual(x, mixed, w_out, "pool_out")


def _forget_cumsum_body(fl_ref, b_ref, cum_ref, cumt_ref, carry_ref, *, tm):
    @pl.when(pl.program_id(0) == 0)
    def _():
        carry_ref[...] = jnp.zeros_like(carry_ref)

    x = fl_ref[...] + b_ref[...]
    logf = -(jnp.maximum(-x, 0.0) + jnp.log1p(jnp.exp(-jnp.abs(x))))
    tri = jnp.where(lax.broadcasted_iota(jnp.int32, (tm, tm), 0)
                    >= lax.broadcasted_iota(jnp.int32, (tm, tm), 1), 1.0, 0.0).astype(BF16)
    cs = carry_ref[...]
    for part in _split3(logf):
        cs = cs + jnp.dot(tri, part, preferred_element_type=F32)
    cum_ref[...] = cs * LOG2_E
    cumt_ref[...] = (cs * LOG2_E).T
    carry_ref[...] = cs[tm - 1:tm, :]


def _forget_cumsum(fl, b_row):
    seq, hp = fl.shape
    tm = _pick(seq, 256)
    return pl.pallas_call(
        functools.partial(_forget_cumsum_body, tm=tm),
        grid=(seq // tm,),
        in_specs=[pl.BlockSpec((tm, hp), lambda i: (i, 0)),
                  pl.BlockSpec((1, hp), lambda i: (0, 0))],
        out_specs=[pl.BlockSpec((tm, hp), lambda i: (i, 0)),
                   pl.BlockSpec((hp, tm), lambda i: (0, i))],
        out_shape=[jax.ShapeDtypeStruct((seq, hp), F32), jax.ShapeDtypeStruct((hp, seq), F32)],
        scratch_shapes=[pltpu.VMEM((1, hp), F32)],
        compiler_params=_params("arbitrary"),
        name="forget_cumsum",
    )(fl, b_row)


def _fox_attn_body(q_ref, k_ref, v_ref, cum_ref, cumt_ref, o_ref,
                   vaug_ref, s_ref, cq_ref, m_ref, acc_ref, *, tq, tk, rs):
    head, qi = pl.program_id(0), pl.program_id(1)
    lw, dh = V7X_LANES, FOX_HEAD_DIM

    @pl.when(qi == 0)
    def _():
        ones_col = jnp.where(lax.broadcasted_iota(jnp.int32, (tk, lw), 1) == 0, 1.0, 0.0).astype(BF16)
        for r0 in range(0, vaug_ref.shape[0], tk):
            vaug_ref[r0:r0 + tk, :dh] = v_ref[r0:r0 + tk, :]
            vaug_ref[r0:r0 + tk, dh:] = ones_col

    q = q_ref[...]
    lane = lax.broadcasted_iota(jnp.int32, cum_ref.shape, 1)
    cq = jnp.sum(jnp.where(lane == head, cum_ref[...], 0.0), axis=-1, keepdims=True)
    cq_ref[...] = jnp.broadcast_to(cq, (tq, lw))
    ck_row = head % V7X_SUBLANES
    m_ref[...] = jnp.full_like(m_ref, -jnp.inf)
    acc_ref[...] = jnp.zeros_like(acc_ref)

    def scores(b, slot, r0=0, nrows=tq):
        start = pl.multiple_of(b * tk, tk)
        ck = cumt_ref[pl.ds(ck_row, 1), pl.ds(start, tk)]
        s_ref[slot, r0:r0 + nrows, :] = lax.dot_general(
            q_ref[r0:r0 + nrows, :], k_ref[pl.ds(start, tk), :], (((1,), (1,)), ((), ())),
            preferred_element_type=F32) - ck

    def update(b, slot, masked, nxt=None):
        start = pl.multiple_of(b * tk, tk)
        vb = vaug_ref[pl.ds(start, tk), :]
        for r0 in range(0, tq, rs):
            if nxt is not None:
                scores(nxt, 1 - slot, r0, rs)
            rows = pl.ds(r0, rs)
            s = s_ref[slot, rows, :]
            if masked:
                qpos = qi * tq + r0 + lax.broadcasted_iota(jnp.int32, (rs, tk), 0)
                kpos = start + lax.broadcasted_iota(jnp.int32, (rs, tk), 1)
                s = jnp.where(qpos >= kpos, s, -jnp.inf)
            cq_rows = cq_ref[rows, :]
            m_prev = m_ref[rows, :]
            m_new = jnp.maximum(m_prev, jnp.max(s, axis=-1, keepdims=True) + cq_rows)
            alpha = jnp.exp2(m_prev - m_new)
            shift = m_new - cq_rows
            p = jnp.concatenate(
                [jnp.exp2(s[:, c * lw:(c + 1) * lw] - shift).astype(BF16) for c in range(tk // lw)], axis=1)
            pv = jnp.dot(p, vb, preferred_element_type=F32)
            for c0 in range(0, 2 * dh, lw):
                acc_ref[rows, c0:c0 + lw] = alpha * acc_ref[rows, c0:c0 + lw] + pv[:, c0:c0 + lw]
            m_ref[rows, :] = m_new

    first_diag = qi * (tq // tk)
    scores(0, 0)

    def pair(kk, carry):
        b = 2 * kk
        update(b, 0, False, nxt=b + 1)
        update(b + 1, 1, False, nxt=b + 2)
        return carry

    lax.fori_loop(0, first_diag // 2, pair, 0)
    update(first_diag, 0, True, nxt=first_diag + 1)
    update(first_diag + 1, 1, True)
    denom = jnp.sum(acc_ref[:, dh:], axis=-1, keepdims=True)
    o_ref[...] = (acc_ref[:, :dh] / denom).astype(o_ref.dtype)


def _fox_attention(qkv, cum, cumt, n_heads, tq):
    seq = qkv.shape[0]
    dh = FOX_HEAD_DIM
    hp = cum.shape[1]
    tk = tq // 2
    rs = _pick(tq, 512, V7X_SUBLANES)
    return pl.pallas_call(
        functools.partial(_fox_attn_body, tq=tq, tk=tk, rs=rs),
        grid=(n_heads, seq // tq),
        in_specs=[pl.BlockSpec((tq, dh), lambda h, i: (i, h)),
                  pl.BlockSpec((seq, dh), lambda h, i: (0, n_heads + h)),
                  pl.BlockSpec((seq, dh), lambda h, i: (0, 2 * n_heads + h)),
                  pl.BlockSpec((tq, hp), lambda h, i: (i, 0)),
                  pl.BlockSpec((V7X_SUBLANES, seq), lambda h, i: (h // V7X_SUBLANES, 0))],
        out_specs=pl.BlockSpec((tq, dh), lambda h, i: (i, h)),
        out_shape=jax.ShapeDtypeStruct((seq, n_heads * dh), BF16),
        scratch_shapes=[pltpu.VMEM((seq, 2 * dh), BF16),
                        pltpu.VMEM((2, tq, tk), F32),
                        pltpu.VMEM((tq, V7X_LANES), F32),
                        pltpu.VMEM((tq, V7X_LANES), F32),
                        pltpu.VMEM((tq, 2 * dh), F32)],
        compiler_params=_params("arbitrary", "arbitrary"),
        name="fox_attention",
    )(qkv, qkv, qkv, cum, cumt)


def _fox_mixer(x, h, w_in, b_f, w_out, tq=1024):
    seq, d = h.shape
    n_heads = d // FOX_HEAD_DIM
    hp = -(-n_heads // V7X_LANES) * V7X_LANES
    w_f = jnp.pad(w_in[:, 3 * d:], ((0, 0), (0, hp - n_heads)))
    b_row = jnp.pad(b_f.astype(F32), (0, hp - n_heads)).reshape(1, hp)
    sm_scale = FOX_HEAD_DIM ** -0.5 * LOG2_E
    col_scale = jnp.concatenate([jnp.full((1, d), sm_scale, F32), jnp.ones((1, 2 * d), F32)], axis=1)
    qkv = _matmul(h, [w_in], [(col_scale, "n")], _ep_colscale, BF16,
                  tm=1024, tn=512, name="fox_qkv", n=3 * d)
    fl = _matmul(h, [w_f], [], _ep_identity, F32, tm=1024, tn=V7X_LANES, name="fox_gate")
    cum, cumt = _forget_cumsum(fl, b_row)
    o = _fox_attention(qkv, cum, cumt, n_heads, _pick(seq, tq))
    return _proj_residual(x, o, w_out, "fox_out")


def _trunk(x, p):
    depth = p["norm_ffn1"].shape[0]
    for i in range(depth):
        h = _rmsnorm(x, p["norm_ffn1"][i], BF16)
        x = _ffn(x, h, p["ffn1_w_gate"][i], p["ffn1_w_up"][i], p["ffn1_w_down"][i])
        h = _rmsnorm(x, p["norm_mix"][i], BF16)
        j, kind = divmod(i, N_MIXERS)
        if kind == 0:
            x = _s5_mixer(x, h, p["s5_w_in"][j], p["s5_lam_re"][j], p["s5_lam_im"][j], p["s5_log_step"][j],
                          p["s5_b_re"][j], p["s5_b_im"][j], p["s5_c_re"][j], p["s5_c_im"][j],
                          p["s5_d"][j], p["s5_w_glu"][j], p["s5_w_out"][j])
        elif kind == 1:
            x = _pool_mixer(x, h, p["pool_w_in"][j], p["pool_w_group"][j], p["pool_scale"][j], p["pool_w_out"][j])
        else:
            x = _fox_mixer(x, h, p["fox_w_in"][j], p["fox_b_f"][j], p["fox_w_out"][j])
        h = _rmsnorm(x, p["norm_ffn2"][i], BF16)
        x = _ffn(x, h, p["ffn2_w_gate"][i], p["ffn2_w_up"][i], p["ffn2_w_down"][i])
    return _rmsnorm(x, p["norm_final"], F32)


def kernel(x, norm_ffn1, ffn1_w_gate, ffn1_w_up, ffn1_w_down, norm_mix, s5_w_in, s5_lam_re, s5_lam_im, s5_log_step, s5_b_re, s5_b_im, s5_c_re, s5_c_im, s5_d, s5_w_glu, s5_w_out, pool_w_in, pool_w_group, pool_scale, pool_w_out, fox_w_in, fox_b_f, fox_w_out, norm_ffn2, ffn2_w_gate, ffn2_w_up, ffn2_w_down, norm_final):
    p = dict(norm_ffn1=norm_ffn1, ffn1_w_gate=ffn1_w_gate, ffn1_w_up=ffn1_w_up, ffn1_w_down=ffn1_w_down,
             norm_mix=norm_mix, s5_w_in=s5_w_in, s5_lam_re=s5_lam_re, s5_lam_im=s5_lam_im,
             s5_log_step=s5_log_step, s5_b_re=s5_b_re, s5_b_im=s5_b_im, s5_c_re=s5_c_re, s5_c_im=s5_c_im,
             s5_d=s5_d, s5_w_glu=s5_w_glu, s5_w_out=s5_w_out, pool_w_in=pool_w_in, pool_w_group=pool_w_group,
             pool_scale=pool_scale, pool_w_out=pool_w_out, fox_w_in=fox_w_in, fox_b_f=fox_b_f,
             fox_w_out=fox_w_out, norm_ffn2=norm_ffn2, ffn2_w_gate=ffn2_w_gate, ffn2_w_up=ffn2_w_up,
             ffn2_w_down=ffn2_w_down, norm_final=norm_final)
    return jnp.stack([_trunk(x[b], p) for b in range(x.shape[0])])
```

```python
import functools
import math

import jax
import jax.numpy as jnp
from jax import lax
from jax.experimental import pallas as pl
from jax.experimental.pallas import tpu as pltpu

RMS_EPS = 1e-6
FFN_RES_WEIGHT = 0.5
N_MIXERS = 3
S5_GROUP = 16
S5_CHUNK = 16
POOL_WINDOWS = (2, 4, 8, 16)
FOX_HEAD_DIM = 128
LOG2_E = 1.0 / math.log(2.0)

V7X_LANES = 128
V7X_SUBLANES = 8
V7X_VMEM_BYTES = 64 * 2**20
VMEM_LIMIT = V7X_VMEM_BYTES - 8 * 2**20

F32 = jnp.float32
BF16 = jnp.bfloat16


def _pick(dim, pref, align=V7X_LANES):
    if dim <= pref:
        return dim
    t = (pref // align) * align
    while t >= align:
        if dim % t == 0:
            return t
        t -= align
    return dim


def _params(*semantics):
    return pltpu.CompilerParams(dimension_semantics=semantics, vmem_limit_bytes=VMEM_LIMIT)


def _split2(x):
    hi = x.astype(BF16)
    return hi, (x - hi.astype(F32)).astype(BF16)


def _split3(x):
    hi = x.astype(BF16)
    r1 = x - hi.astype(F32)
    mid = r1.astype(BF16)
    return hi, mid, (r1 - mid.astype(F32)).astype(BF16)


def _rmsnorm_body(x_ref, g_ref, o_ref):
    x = x_ref[...]
    ms = jnp.mean(x * x, axis=-1, keepdims=True)
    o_ref[...] = (x * lax.rsqrt(ms + RMS_EPS) * g_ref[...]).astype(o_ref.dtype)


def _rmsnorm(x, g, out_dtype):
    rows, d = x.shape
    tr = _pick(rows, 256)
    return pl.pallas_call(
        _rmsnorm_body,
        grid=(rows // tr,),
        in_specs=[pl.BlockSpec((tr, d), lambda i: (i, 0)),
                  pl.BlockSpec((1, d), lambda i: (0, 0))],
        out_specs=pl.BlockSpec((tr, d), lambda i: (i, 0)),
        out_shape=jax.ShapeDtypeStruct((rows, d), out_dtype),
        compiler_params=_params("parallel"),
        name="rmsnorm",
    )(x, g.reshape(1, d).astype(F32))


def _mm_body(*refs, n_b, n_e, sub, epilogue):
    a_ref = refs[0]
    b_refs = refs[1:1 + n_b]
    e_refs = refs[1 + n_b:1 + n_b + n_e]
    o_ref = refs[1 + n_b + n_e]
    a = a_ref[...]
    for c0 in range(0, o_ref.shape[1], sub):
        cols = slice(c0, c0 + sub)
        parts = [jnp.dot(a, b[:, cols], preferred_element_type=F32) for b in b_refs]
        o_ref[:, cols] = epilogue(parts, [e[:, cols] for e in e_refs]).astype(o_ref.dtype)


def _matmul(a, bs, extras, epilogue, out_dtype, *, tm, tn, name, n=None):
    m, kdim = a.shape
    n = bs[0][0].shape[2] if n is None else n
    tm, tn = _pick(m, tm), _pick(n, tn)
    sub = _pick(tn, 2 * V7X_LANES)
    in_specs = [pl.BlockSpec((tm, kdim), lambda i, j: (i, 0))]
    for _, layer in bs:
        in_specs.append(pl.BlockSpec((None, kdim, tn), lambda i, j, layer=layer: (layer, 0, j)))
    for _, kind in extras:
        if kind == "mn":
            in_specs.append(pl.BlockSpec((tm, tn), lambda i, j: (i, j)))
        else:
            in_specs.append(pl.BlockSpec((1, tn), lambda i, j: (0, j)))
    return pl.pallas_call(
        functools.partial(_mm_body, n_b=len(bs), n_e=len(extras), sub=sub, epilogue=epilogue),
        grid=(m // tm, n // tn),
        in_specs=in_specs,
        out_specs=pl.BlockSpec((tm, tn), lambda i, j: (i, j)),
        out_shape=jax.ShapeDtypeStruct((m, n), out_dtype),
        compiler_params=_params("parallel", "parallel"),
        name=name,
    )(a, *[w for w, _ in bs], *[e for e, _ in extras])


def _ep_identity(parts, extras):
    return parts[0]


def _ep_colscale(parts, extras):
    return parts[0] * extras[0]


def _ep_swiglu(parts, extras):
    gate, up = parts
    return gate * (1.0 / (1.0 + jnp.exp(-gate))) * up


def _ep_residual(weight, parts, extras):
    return extras[0] + weight * parts[0]


def _ep_glu(parts, extras):
    z = extras[0].astype(F32)
    return z * (1.0 / (1.0 + jnp.exp(-parts[0])))


def _gelu_tanh(x):
    c = math.sqrt(2.0 / math.pi)
    return x * (0.5 * (1.0 + jnp.tanh(c * (x + 0.044715 * (x * x * x)))))


def _ffn(x, h, w_gate, w_up, w_down):
    act = _matmul(h, [w_gate, w_up], [], _ep_swiglu, BF16, tm=1024, tn=512, name="ffn_up")
    return _matmul(act, [w_down], [(x, "mn")], functools.partial(_ep_residual, FFN_RES_WEIGHT), F32,
                   tm=1024, tn=512, name="ffn_down")


def _proj_residual(x, a, w, name):
    return _matmul(a, [w], [(x, "mn")], functools.partial(_ep_residual, 1.0), F32,
                   tm=1024, tn=1024, name=name)


def _s5_operators(lam_re, lam_im, log_step, b_re, b_im, c_re, c_im, d_skip):
    g, p = lam_re.shape
    n, t = S5_GROUP, S5_CHUNK
    lam_re, lam_im = lam_re.astype(F32), lam_im.astype(F32)
    step = jnp.exp(log_step.astype(F32))[:, None]
    er, ei = lam_re * step, lam_im * step

    def a_pow(k):
        k = jnp.asarray(k, F32)[None, :, None]
        mag = jnp.exp(er[:, None, :] * k)
        return mag * jnp.cos(ei[:, None, :] * k), mag * jnp.sin(ei[:, None, :] * k)

    def cmul(xr, xi, yr, yi):
        return xr * yr - xi * yi, xr * yi + xi * yr

    a_re, a_im = a_pow([1.0])
    xr, xi = a_re[:, 0] - 1.0, a_im[:, 0]
    den = lam_re * lam_re + lam_im * lam_im
    qr, qi = (xr * lam_re + xi * lam_im) / den, (xi * lam_re - xr * lam_im) / den
    bt_re, bt_im = b_re.astype(F32).transpose(0, 2, 1), b_im.astype(F32).transpose(0, 2, 1)
    bb_re, bb_im = cmul(qr[:, None, :], qi[:, None, :], bt_re, bt_im)
    cc_re, cc_im = c_re.astype(F32), c_im.astype(F32)
    steps = jnp.arange(t, dtype=F32)
    half = t // 2

    def outer(pw, mat):
        r, i = cmul(pw[0][:, :, None, :], pw[1][:, :, None, :], mat[0][:, None], mat[1][:, None])
        return r.reshape(g, t * n, p), i.reshape(g, t * n, p)

    fb_re, fb_im = outer(a_pow(half - steps), (bb_re, bb_im))
    fc_re, fc_im = outer(a_pow(steps - half), (cc_re, cc_im))
    fb = jnp.concatenate([fb_re, -fb_im], axis=-1)
    fc = jnp.concatenate([fc_re, fc_im], axis=-1).transpose(0, 2, 1)
    v_re, v_im = outer(a_pow(t - 1.0 - steps), (bb_re, bb_im))
    v_mat = jnp.concatenate([v_re, v_im], axis=-1).astype(BF16)
    v_swap = jnp.concatenate([v_im, v_re], axis=-1).astype(BF16)
    w_re, w_im = outer(a_pow(steps + 1.0), (cc_re, cc_im))
    w_mat = jnp.concatenate([w_re, -w_im], axis=-1).transpose(0, 2, 1).astype(BF16)
    at_re, at_im = a_pow([float(t)])
    a1 = jnp.concatenate([at_re[:, 0], at_re[:, 0]], axis=-1)
    a2 = jnp.concatenate([-at_im[:, 0], at_im[:, 0]], axis=-1)
    d_row = jnp.tile(d_skip.astype(F32).reshape(g, n), (1, t))
    return fb, fc, v_mat, v_swap, w_mat, d_row, a1, a2


def _s5_core_body(u_ref, fb_ref, fc_ref, v_ref, vs_ref, w_ref, d_ref, a1_ref, a2_ref, z_ref,
                  perm_ref, m_ref, x_ref, loc_ref, locs_ref, prev_ref, s_ref, ss_ref, *, gb, cb, t, n):
    col, blk = pl.program_id(0), pl.program_id(1)
    width = t * n
    lanes = gb * n
    wide = t * lanes
    ln, lw = n.bit_length() - 1, width.bit_length() - 1

    @pl.when(jnp.logical_and(col == 0, blk == 0))
    def _():
        strip = 256
        c = lax.broadcasted_iota(jnp.int32, (strip, wide), 1)
        src = (((c >> ln) & (t - 1)) * lanes) + ((c >> lw) * n) + (c & (n - 1))
        for r0 in range(0, wide, strip):
            r = lax.broadcasted_iota(jnp.int32, (strip, wide), 0) + r0
            perm_ref[r0:r0 + strip, :] = jnp.where(r == src, 1.0, 0.0).astype(BF16)

    @pl.when(blk == 0)
    def _():
        rj = lax.broadcasted_iota(jnp.int32, (width, width), 0)
        ct = lax.broadcasted_iota(jnp.int32, (width, width), 1)
        for g in range(gb):
            fbh, fbl = _split2(fb_ref[g])
            fch, fcl = _split2(fc_ref[g])
            mm = (jnp.dot(fbh, fch, preferred_element_type=F32) + jnp.dot(fbh, fcl, preferred_element_type=F32)
                  + jnp.dot(fbl, fch, preferred_element_type=F32))
            mm = jnp.where((ct >> ln) >= (rj >> ln), mm, 0.0) + jnp.where(rj == ct, d_ref[pl.ds(g, 1), :], 0.0)
            m_ref[g] = mm.astype(BF16)
        s_ref[...] = jnp.zeros_like(s_ref)
        ss_ref[...] = jnp.zeros_like(ss_ref)

    x_ref[...] = u_ref[...].astype(F32)
    xcat = jnp.concatenate([x_ref[pl.ds(tt, cb, stride=t), :] for tt in range(t)], axis=1).astype(BF16)
    ug_all = jnp.dot(xcat, perm_ref[...], preferred_element_type=F32).astype(BF16)
    for g in range(gb):
        ug = ug_all[:, g * width:(g + 1) * width]
        loc_ref[g * cb:(g + 1) * cb, :] = jnp.dot(ug, v_ref[g], preferred_element_type=F32)
        locs_ref[g * cb:(g + 1) * cb, :] = jnp.dot(ug, vs_ref[g], preferred_element_type=F32)
    a1, a2 = a1_ref[...], a2_ref[...]

    def step(c, carry):
        s, ss = carry
        rows = pl.ds(c, gb, stride=cb)
        prev_ref[rows, :] = s
        s_new = a1 * s + a2 * ss + loc_ref[rows, :]
        ss_new = a1 * ss - a2 * s + locs_ref[rows, :]
        return s_new, ss_new

    s_ref[...], ss_ref[...] = lax.fori_loop(0, cb, step, (s_ref[...], ss_ref[...]), unroll=V7X_SUBLANES)
    zs = []
    for g in range(gb):
        ug = ug_all[:, g * width:(g + 1) * width]
        y = jnp.dot(ug, m_ref[g], preferred_element_type=F32)
        y += jnp.dot(prev_ref[g * cb:(g + 1) * cb, :].astype(BF16), w_ref[g], preferred_element_type=F32)
        zs.append(_gelu_tanh(y).astype(BF16))
    zcat = lax.dot_general(jnp.concatenate(zs, axis=1), perm_ref[...], (((1,), (1,)), ((), ())),
                           preferred_element_type=F32)
    for tt in range(t):
        x_ref[pl.ds(tt, cb, stride=t), :] = zcat[:, tt * lanes:(tt + 1) * lanes]
    z_ref[...] = x_ref[...].astype(z_ref.dtype)


def _s5_core(u, ops):
    fb, fc, v_mat, v_swap, w_mat, d_row, a1, a2 = ops
    seq, d = u.shape
    t, n = S5_CHUNK, S5_GROUP
    gb = V7X_LANES // n
    width, state = t * n, a1.shape[-1]
    chunks = seq // t
    cb = _pick(chunks, 512, V7X_SUBLANES)
    wide = t * V7X_LANES
    blk3 = lambda d1, d2: pl.BlockSpec((gb, d1, d2), lambda j, i: (j, 0, 0))
    blk2 = lambda d1: pl.BlockSpec((gb, d1), lambda j, i: (j, 0))
    return pl.pallas_call(
        functools.partial(_s5_core_body, gb=gb, cb=cb, t=t, n=n),
        grid=(d // V7X_LANES, chunks // cb),
        in_specs=[pl.BlockSpec((cb * t, V7X_LANES), lambda j, i: (i, j)),
                  blk3(width, state), blk3(state, width), blk3(width, state), blk3(width, state),
                  blk3(state, width), blk2(width), blk2(state), blk2(state)],
        out_specs=pl.BlockSpec((cb * t, V7X_LANES), lambda j, i: (i, j)),
        out_shape=jax.ShapeDtypeStruct((seq, d), BF16),
        scratch_shapes=[pltpu.VMEM((wide, wide), BF16),
                        pltpu.VMEM((gb, width, width), BF16),
                        pltpu.VMEM((cb * t, V7X_LANES), F32),
                        pltpu.VMEM((gb * cb, state), F32),
                        pltpu.VMEM((gb * cb, state), F32),
                        pltpu.VMEM((gb * cb, state), F32),
                        pltpu.VMEM((gb, state), F32),
                        pltpu.VMEM((gb, state), F32)],
        compiler_params=_params("arbitrary", "arbitrary"),
        name="s5_core",
    )(u, fb, fc, v_mat, v_swap, w_mat, d_row, a1, a2)


def _s5_mixer(x, h, w_in, lam_re, lam_im, log_step, b_re, b_im, c_re, c_im, d_skip, w_glu, w_out):
    ops = _s5_operators(lam_re, lam_im, log_step, b_re, b_im, c_re, c_im, d_skip)
    u = _matmul(h, [w_in], [], _ep_identity, BF16, tm=1024, tn=1024, name="s5_in")
    z = _s5_core(u, ops)
    gated = _matmul(z, [w_glu], [(z, "mn")], _ep_glu, BF16, tm=1024, tn=1024, name="s5_glu")
    return _proj_residual(x, gated, w_out, "s5_out")


def _pool_band_body(u_ref, o_ref, hist_ref, *, tm, tn, hist, group_cols):
    j, i = pl.program_id(0), pl.program_id(1)

    @pl.when(i == 0)
    def _():
        hist_ref[...] = jnp.zeros_like(hist_ref)

    u = u_ref[...]
    ub = u.astype(BF16)
    ext = jnp.concatenate([hist_ref[...], ub], axis=0)
    gid = (j * tn) // group_cols
    win = jnp.int32(POOL_WINDOWS[0])
    for k, w in enumerate(POOL_WINDOWS[1:], start=1):
        win = jnp.where(gid == k, jnp.int32(w), win)
    lag = (lax.broadcasted_iota(jnp.int32, (tm, hist + tm), 0) + hist
           - lax.broadcasted_iota(jnp.int32, (tm, hist + tm), 1))
    band = jnp.where(lag >= 0, jnp.where(lag < win, 1.0, 0.0), 0.0).astype(BF16)
    wsum = jnp.dot(band, ext, preferred_element_type=F32)
    count = jnp.minimum(i * tm + lax.broadcasted_iota(jnp.int32, (tm, 1), 0) + 1, win).astype(F32)
    o_ref[...] = (wsum / count - u).astype(o_ref.dtype)
    hist_ref[...] = ub[tm - hist:, :]


def _pool_band(u, n_groups):
    seq, d = u.shape
    group_cols = d // n_groups
    tm = _pick(seq, 512)
    tn = _pick(group_cols, 512)
    hist = min(V7X_LANES, tm)
    assert hist >= max(POOL_WINDOWS)
    return pl.pallas_call(
        functools.partial(_pool_band_body, tm=tm, tn=tn, hist=hist, group_cols=group_cols),
        grid=(d // tn, seq // tm),
        in_specs=[pl.BlockSpec((tm, tn), lambda j, i: (i, j))],
        out_specs=pl.BlockSpec((tm, tn), lambda j, i: (i, j)),
        out_shape=jax.ShapeDtypeStruct((seq, d), BF16),
        scratch_shapes=[pltpu.VMEM((hist, tn), BF16)],
        compiler_params=_params("parallel", "arbitrary"),
        name="pool_band",
    )(u)


def _pool_group_body(a_ref, w_ref, s_ref, o_ref):
    o_ref[...] = (jnp.dot(a_ref[...], w_ref[...], preferred_element_type=F32) * s_ref[...]).astype(o_ref.dtype)


def _pool_group(diff, w_group, scale):
    seq, d = diff.shape
    n_groups, gc, _ = w_group.shape
    tm = _pick(seq, 1024)
    return pl.pallas_call(
        _pool_group_body,
        grid=(n_groups, seq // tm),
        in_specs=[pl.BlockSpec((tm, gc), lambda g, i: (i, g)),
                  pl.BlockSpec((None, gc, gc), lambda g, i: (g, 0, 0)),
                  pl.BlockSpec((1, gc), lambda g, i: (0, g))],
        out_specs=pl.BlockSpec((tm, gc), lambda g, i: (i, g)),
        out_shape=jax.ShapeDtypeStruct((seq, d), BF16),
        compiler_params=_params("parallel", "parallel"),
        name="pool_group",
    )(diff, w_group.astype(BF16), scale.reshape(1, d).astype(F32))


def _pool_mixer(x, h, w_in, w_group, scale, w_out):
    u = _matmul(h, [w_in], [], _ep_identity, F32, tm=1024, tn=1024, name="pool_in")
    diff = _pool_band(u, len(POOL_WINDOWS))
    mixed = _pool_group(diff, w_group, scale)
    return _proj_residual(x, mixed, w_out, "pool_out")


def _forget_cumsum_body(fl_ref, b_ref, cum_ref, cumt_ref, carry_ref, *, tm):
    @pl.when(pl.program_id(0) == 0)
    def _():
        carry_ref[...] = jnp.zeros_like(carry_ref)

    x = fl_ref[...] + b_ref[...]
    logf = -(jnp.maximum(-x, 0.0) + jnp.log1p(jnp.exp(-jnp.abs(x))))
    tri = jnp.where(lax.broadcasted_iota(jnp.int32, (tm, tm), 0)
                    >= lax.broadcasted_iota(jnp.int32, (tm, tm), 1), 1.0, 0.0).astype(BF16)
    cs = carry_ref[...]
    for part in _split3(logf):
        cs = cs + jnp.dot(tri, part, preferred_element_type=F32)
    cum_ref[...] = cs * LOG2_E
    cumt_ref[...] = (cs * LOG2_E).T
    carry_ref[...] = cs[tm - 1:tm, :]


def _forget_cumsum(fl, b_row):
    seq, hp = fl.shape
    tm = _pick(seq, 256)
    return pl.pallas_call(
        functools.partial(_forget_cumsum_body, tm=tm),
        grid=(seq // tm,),
        in_specs=[pl.BlockSpec((tm, hp), lambda i: (i, 0)),
                  pl.BlockSpec((1, hp), lambda i: (0, 0))],
        out_specs=[pl.BlockSpec((tm, hp), lambda i: (i, 0)),
                   pl.BlockSpec((hp, tm), lambda i: (0, i))],
        out_shape=[jax.ShapeDtypeStruct((seq, hp), F32), jax.ShapeDtypeStruct((hp, seq), F32)],
        scratch_shapes=[pltpu.VMEM((1, hp), F32)],
        compiler_params=_params("arbitrary"),
        name="forget_cumsum",
    )(fl, b_row)


def _fox_attn_body(q_ref, k_ref, v_ref, cum_ref, cumt_ref, o_ref,
                   vaug_ref, s_ref, cq_ref, m_ref, acc_ref, *, tq, tk, rs):
    head, qi = pl.program_id(0), pl.program_id(1)
    lw, dh = V7X_LANES, FOX_HEAD_DIM

    @pl.when(qi == 0)
    def _():
        ones_col = jnp.where(lax.broadcasted_iota(jnp.int32, (tk, lw), 1) == 0, 1.0, 0.0).astype(BF16)
        for r0 in range(0, vaug_ref.shape[0], tk):
            vaug_ref[r0:r0 + tk, :dh] = v_ref[r0:r0 + tk, :]
            vaug_ref[r0:r0 + tk, dh:] = ones_col

    q = q_ref[...]
    lane = lax.broadcasted_iota(jnp.int32, cum_ref.shape, 1)
    cq = jnp.sum(jnp.where(lane == head, cum_ref[...], 0.0), axis=-1, keepdims=True)
    cq_ref[...] = jnp.broadcast_to(cq, (tq, lw))
    ck_row = head % V7X_SUBLANES
    m_ref[...] = jnp.full_like(m_ref, -jnp.inf)
    acc_ref[...] = jnp.zeros_like(acc_ref)

    def scores(b, slot, r0=0, nrows=tq):
        start = pl.multiple_of(b * tk, tk)
        ck = cumt_ref[pl.ds(ck_row, 1), pl.ds(start, tk)]
        s_ref[slot, r0:r0 + nrows, :] = lax.dot_general(
            q_ref[r0:r0 + nrows, :], k_ref[pl.ds(start, tk), :], (((1,), (1,)), ((), ())),
            preferred_element_type=F32) - ck

    def update(b, slot, masked, nxt=None):
        start = pl.multiple_of(b * tk, tk)
        vb = vaug_ref[pl.ds(start, tk), :]
        for r0 in range(0, tq, rs):
            if nxt is not None:
                scores(nxt, 1 - slot, r0, rs)
            rows = pl.ds(r0, rs)
            s = s_ref[slot, rows, :]
            if masked:
                qpos = qi * tq + r0 + lax.broadcasted_iota(jnp.int32, (rs, tk), 0)
                kpos = start + lax.broadcasted_iota(jnp.int32, (rs, tk), 1)
                s = jnp.where(qpos >= kpos, s, -jnp.inf)
            cq_rows = cq_ref[rows, :]
            m_prev = m_ref[rows, :]
            m_new = jnp.maximum(m_prev, jnp.max(s, axis=-1, keepdims=True) + cq_rows)
            alpha = jnp.exp2(m_prev - m_new)
            shift = m_new - cq_rows
            p = jnp.concatenate(
                [jnp.exp2(s[:, c * lw:(c + 1) * lw] - shift).astype(BF16) for c in range(tk // lw)], axis=1)
            pv = jnp.dot(p, vb, preferred_element_type=F32)
            for c0 in range(0, 2 * dh, lw):
                acc_ref[rows, c0:c0 + lw] = alpha * acc_ref[rows, c0:c0 + lw] + pv[:, c0:c0 + lw]
            m_ref[rows, :] = m_new

    first_diag = qi * (tq // tk)
    scores(0, 0)

    def pair(kk, carry):
        b = 2 * kk
        update(b, 0, False, nxt=b + 1)
        update(b + 1, 1, False, nxt=b + 2)
        return carry

    lax.fori_loop(0, first_diag // 2, pair, 0)
    update(first_diag, 0, True, nxt=first_diag + 1)
    update(first_diag + 1, 1, True)
    denom = jnp.sum(acc_ref[:, dh:], axis=-1, keepdims=True)
    o_ref[...] = (acc_ref[:, :dh] / denom).astype(o_ref.dtype)


def _fox_attention(qkv, cum, cumt, n_heads, tq):
    seq = qkv.shape[0]
    dh = FOX_HEAD_DIM
    hp = cum.shape[1]
    tk = tq // 2
    rs = _pick(tq, 512, V7X_SUBLANES)
    return pl.pallas_call(
        functools.partial(_fox_attn_body, tq=tq, tk=tk, rs=rs),
        grid=(n_heads, seq // tq),
        in_specs=[pl.BlockSpec((tq, dh), lambda h, i: (i, h)),
                  pl.BlockSpec((seq, dh), lambda h, i: (0, n_heads + h)),
                  pl.BlockSpec((seq, dh), lambda h, i: (0, 2 * n_heads + h)),
                  pl.BlockSpec((tq, hp), lambda h, i: (i, 0)),
                  pl.BlockSpec((V7X_SUBLANES, seq), lambda h, i: (h // V7X_SUBLANES, 0))],
        out_specs=pl.BlockSpec((tq, dh), lambda h, i: (i, h)),
        out_shape=jax.ShapeDtypeStruct((seq, n_heads * dh), BF16),
        scratch_shapes=[pltpu.VMEM((seq, 2 * dh), BF16),
                        pltpu.VMEM((2, tq, tk), F32),
                        pltpu.VMEM((tq, V7X_LANES), F32),
                        pltpu.VMEM((tq, V7X_LANES), F32),
                        pltpu.VMEM((tq, 2 * dh), F32)],
        compiler_params=_params("arbitrary", "arbitrary"),
        name="fox_attention",
    )(qkv, qkv, qkv, cum, cumt)


def _fox_mixer(x, h, w_in, b_f, w_out, tq=1024):
    seq, d = h.shape
    n_heads = d // FOX_HEAD_DIM
    hp = -(-n_heads // V7X_LANES) * V7X_LANES
    w_stack, layer = w_in
    w_f = jnp.pad(w_stack[layer][:, 3 * d:], ((0, 0), (0, hp - n_heads)))[None]
    b_row = jnp.pad(b_f.astype(F32), (0, hp - n_heads)).reshape(1, hp)
    sm_scale = FOX_HEAD_DIM ** -0.5 * LOG2_E
    col_scale = jnp.concatenate([jnp.full((1, d), sm_scale, F32), jnp.ones((1, 2 * d), F32)], axis=1)
    qkv = _matmul(h, [w_in], [(col_scale, "n")], _ep_colscale, BF16,
                  tm=1024, tn=1024, name="fox_qkv", n=3 * d)
    fl = _matmul(h, [(w_f, 0)], [], _ep_identity, F32, tm=1024, tn=V7X_LANES, name="fox_gate")
    cum, cumt = _forget_cumsum(fl, b_row)
    o = _fox_attention(qkv, cum, cumt, n_heads, _pick(seq, tq))
    return _proj_residual(x, o, w_out, "fox_out")


MATMUL_WEIGHTS = ("ffn1_w_gate", "ffn1_w_up", "ffn1_w_down", "ffn2_w_gate", "ffn2_w_up", "ffn2_w_down",
                  "s5_w_in", "s5_w_glu", "s5_w_out", "pool_w_in", "pool_w_out", "fox_w_in", "fox_w_out")


def _trunk(x, p):
    depth = p["norm_ffn1"].shape[0]
    for i in range(depth):
        h = _rmsnorm(x, p["norm_ffn1"][i], BF16)
        x = _ffn(x, h, (p["ffn1_w_gate"], i), (p["ffn1_w_up"], i), (p["ffn1_w_down"], i))
        h = _rmsnorm(x, p["norm_mix"][i], BF16)
        j, kind = divmod(i, N_MIXERS)
        if kind == 0:
            x = _s5_mixer(x, h, (p["s5_w_in"], j), p["s5_lam_re"][j], p["s5_lam_im"][j], p["s5_log_step"][j],
                          p["s5_b_re"][j], p["s5_b_im"][j], p["s5_c_re"][j], p["s5_c_im"][j],
                          p["s5_d"][j], (p["s5_w_glu"], j), (p["s5_w_out"], j))
        elif kind == 1:
            x = _pool_mixer(x, h, (p["pool_w_in"], j), p["pool_w_group"][j], p["pool_scale"][j],
                            (p["pool_w_out"], j))
        else:
            x = _fox_mixer(x, h, (p["fox_w_in"], j), p["fox_b_f"][j], (p["fox_w_out"], j))
        h = _rmsnorm(x, p["norm_ffn2"][i], BF16)
        x = _ffn(x, h, (p["ffn2_w_gate"], i), (p["ffn2_w_up"], i), (p["ffn2_w_down"], i))
    return _rmsnorm(x, p["norm_final"], F32)


def kernel(x, norm_ffn1, ffn1_w_gate, ffn1_w_up, ffn1_w_down, norm_mix, s5_w_in, s5_lam_re, s5_lam_im, s5_log_step, s5_b_re, s5_b_im, s5_c_re, s5_c_im, s5_d, s5_w_glu, s5_w_out, pool_w_in, pool_w_group, pool_scale, pool_w_out, fox_w_in, fox_b_f, fox_w_out, norm_ffn2, ffn2_w_gate, ffn2_w_up, ffn2_w_down, norm_final):
    p = dict(norm_ffn1=norm_ffn1, ffn1_w_gate=ffn1_w_gate, ffn1_w_up=ffn1_w_up, ffn1_w_down=ffn1_w_down,
             norm_mix=norm_mix, s5_w_in=s5_w_in, s5_lam_re=s5_lam_re, s5_lam_im=s5_lam_im,
             s5_log_step=s5_log_step, s5_b_re=s5_b_re, s5_b_im=s5_b_im, s5_c_re=s5_c_re, s5_c_im=s5_c_im,
             s5_d=s5_d, s5_w_glu=s5_w_glu, s5_w_out=s5_w_out, pool_w_in=pool_w_in, pool_w_group=pool_w_group,
             pool_scale=pool_scale, pool_w_out=pool_w_out, fox_w_in=fox_w_in, fox_b_f=fox_b_f,
             fox_w_out=fox_w_out, norm_ffn2=norm_ffn2, ffn2_w_gate=ffn2_w_gate, ffn2_w_up=ffn2_w_up,
             ffn2_w_down=ffn2_w_down, norm_final=norm_final)
    p = {k: (v.astype(BF16) if k in MATMUL_WEIGHTS else v) for k, v in p.items()}
    return jnp.stack([_trunk(x[b], p) for b in range(x.shape[0])])
```

```python
import functools
import math

import jax
import jax.numpy as jnp
from jax import lax
from jax.experimental import pallas as pl
from jax.experimental.pallas import tpu as pltpu

RMS_EPS = 1e-6
FFN_RES_WEIGHT = 0.5
N_MIXERS = 3
S5_GROUP = 16
S5_CHUNK = 16
POOL_WINDOWS = (2, 4, 8, 16)
FOX_HEAD_DIM = 128
LOG2_E = 1.0 / math.log(2.0)

V7X_LANES = 128
V7X_SUBLANES = 8
V7X_VMEM_BYTES = 64 * 2**20
VMEM_LIMIT = V7X_VMEM_BYTES - 8 * 2**20

F32 = jnp.float32
BF16 = jnp.bfloat16


def _pick(dim, pref, align=V7X_LANES):
    if dim <= pref:
        return dim
    t = (pref // align) * align
    while t >= align:
        if dim % t == 0:
            return t
        t -= align
    return dim


def _params(*semantics):
    return pltpu.CompilerParams(dimension_semantics=semantics, vmem_limit_bytes=VMEM_LIMIT)


def _split2(x):
    hi = x.astype(BF16)
    return hi, (x - hi.astype(F32)).astype(BF16)


def _split3(x):
    hi = x.astype(BF16)
    r1 = x - hi.astype(F32)
    mid = r1.astype(BF16)
    return hi, mid, (r1 - mid.astype(F32)).astype(BF16)


def _rmsnorm_body(x_ref, g_ref, o_ref):
    x = x_ref[...]
    ms = jnp.mean(x * x, axis=-1, keepdims=True)
    o_ref[...] = (x * lax.rsqrt(ms + RMS_EPS) * g_ref[...]).astype(o_ref.dtype)


def _rmsnorm(x, g, out_dtype):
    rows, d = x.shape
    tr = _pick(rows, 256)
    return pl.pallas_call(
        _rmsnorm_body,
        grid=(rows // tr,),
        in_specs=[pl.BlockSpec((tr, d), lambda i: (i, 0)),
                  pl.BlockSpec((1, d), lambda i: (0, 0))],
        out_specs=pl.BlockSpec((tr, d), lambda i: (i, 0)),
        out_shape=jax.ShapeDtypeStruct((rows, d), out_dtype),
        compiler_params=_params("parallel"),
        name="rmsnorm",
    )(x, g.reshape(1, d).astype(F32))


def _mm_body(*refs, n_b, kinds, sub, epilogue, stats):
    a_ref = refs[0]
    b_refs = refs[1:1 + n_b]
    e_refs = refs[1 + n_b:1 + n_b + len(kinds)]
    o_ref = refs[1 + n_b + len(kinds)]
    a = a_ref[...]
    ssq = None
    for c0 in range(0, o_ref.shape[1], sub):
        cols = slice(c0, c0 + sub)
        parts = [jnp.dot(a, b[:, cols], preferred_element_type=F32) for b in b_refs]
        val = epilogue(parts, [e[...] if kind == "m" else e[:, cols] for e, kind in zip(e_refs, kinds)])
        o_ref[:, cols] = val.astype(o_ref.dtype)
        if stats:
            xb_ref, r_ref, ssq_ref = refs[2 + n_b + len(kinds):]
            xb_ref[:, cols] = val.astype(BF16)
            part = jnp.sum(val * val, axis=-1, keepdims=True)
            ssq = part if ssq is None else ssq + part
    if stats:
        j, nj = pl.program_id(1), pl.num_programs(1)

        @pl.when(j == 0)
        def _():
            ssq_ref[...] = ssq

        @pl.when(j > 0)
        def _():
            ssq_ref[...] += ssq

        @pl.when(j == nj - 1)
        def _():
            r_ref[...] = lax.rsqrt(ssq_ref[...] / stats + RMS_EPS)


def _matmul(a, bs, extras, epilogue, out_dtype, *, tm, tn, name, n=None, stats=False):
    m, kdim = a.shape
    n = bs[0][0].shape[2] if n is None else n
    tm, tn = _pick(m, tm), _pick(n, tn)
    sub = _pick(tn, 2 * V7X_LANES)
    in_specs = [pl.BlockSpec((tm, kdim), lambda i, j: (i, 0))]
    for _, layer in bs:
        in_specs.append(pl.BlockSpec((None, kdim, tn), lambda i, j, layer=layer: (layer, 0, j)))
    for _, kind in extras:
        if kind == "mn":
            in_specs.append(pl.BlockSpec((tm, tn), lambda i, j: (i, j)))
        elif kind == "n":
            in_specs.append(pl.BlockSpec((1, tn), lambda i, j: (0, j)))
        else:
            in_specs.append(pl.BlockSpec((tm, 1), lambda i, j: (i, 0)))
    out_specs = pl.BlockSpec((tm, tn), lambda i, j: (i, j))
    out_shape = jax.ShapeDtypeStruct((m, n), out_dtype)
    scratch = []
    if stats:
        out_specs = [out_specs, pl.BlockSpec((tm, tn), lambda i, j: (i, j)), pl.BlockSpec((tm, 1), lambda i, j: (i, 0))]
        out_shape = [out_shape, jax.ShapeDtypeStruct((m, n), BF16), jax.ShapeDtypeStruct((m, 1), F32)]
        scratch = [pltpu.VMEM((tm, 1), F32)]
    return pl.pallas_call(
        functools.partial(_mm_body, n_b=len(bs), kinds=tuple(k for _, k in extras), sub=sub,
                          epilogue=epilogue, stats=float(n) if stats else 0.0),
        grid=(m // tm, n // tn),
        in_specs=in_specs,
        out_specs=out_specs,
        out_shape=out_shape,
        scratch_shapes=scratch,
        compiler_params=_params("parallel", "arbitrary"),
        name=name,
    )(a, *[w for w, _ in bs], *[e for e, _ in extras])


def _ep_identity(parts, extras):
    return parts[0]


def _ep_colscale(parts, extras):
    return parts[0] * extras[0]


def _ep_swiglu(parts, extras):
    gate, up = parts
    return gate * (1.0 / (1.0 + jnp.exp(-gate))) * up


def _ep_residual(weight, parts, extras):
    return extras[0] + weight * parts[0]


def _ep_glu(parts, extras):
    z = extras[0].astype(F32)
    return z * (1.0 / (1.0 + jnp.exp(-parts[0])))


def _gelu_tanh(x):
    c = math.sqrt(2.0 / math.pi)
    return x * (0.5 * (1.0 + jnp.tanh(c * (x + 0.044715 * (x * x * x)))))


def _normed(epilogue):
    def scaled(parts, extras):
        return epilogue([extras[-1] * p for p in parts], extras[:-1])
    return scaled


def _row_stats_body(x_ref, xb_ref, r_ref):
    x = x_ref[...]
    xb_ref[...] = x.astype(BF16)
    r_ref[...] = lax.rsqrt(jnp.mean(x * x, axis=-1, keepdims=True) + RMS_EPS)


def _row_stats(x):
    rows, d = x.shape
    tr = _pick(rows, 256)
    return pl.pallas_call(
        _row_stats_body,
        grid=(rows // tr,),
        in_specs=[pl.BlockSpec((tr, d), lambda i: (i, 0))],
        out_specs=[pl.BlockSpec((tr, d), lambda i: (i, 0)), pl.BlockSpec((tr, 1), lambda i: (i, 0))],
        out_shape=[jax.ShapeDtypeStruct((rows, d), BF16), jax.ShapeDtypeStruct((rows, 1), F32)],
        compiler_params=_params("parallel"),
        name="row_stats",
    )(x)


def _ffn(xs, w_gate, w_up, w_down):
    x, xb, r = xs
    act = _matmul(xb, [w_gate, w_up], [(r, "m")], _normed(_ep_swiglu), BF16, tm=1024, tn=512, name="ffn_up")
    return _matmul(act, [w_down], [(x, "mn")], functools.partial(_ep_residual, FFN_RES_WEIGHT), F32,
                   tm=1024, tn=512, name="ffn_down", stats=True)


def _proj_residual(x, a, w, name):
    return _matmul(a, [w], [(x, "mn")], functools.partial(_ep_residual, 1.0), F32,
                   tm=1024, tn=512, name=name, stats=True)


def _s5_operators(lam_re, lam_im, log_step, b_re, b_im, c_re, c_im, d_skip):
    g, p = lam_re.shape
    n, t = S5_GROUP, S5_CHUNK
    lam_re, lam_im = lam_re.astype(F32), lam_im.astype(F32)
    step = jnp.exp(log_step.astype(F32))[:, None]
    er, ei = lam_re * step, lam_im * step

    def a_pow(k):
        k = jnp.asarray(k, F32)[None, :, None]
        mag = jnp.exp(er[:, None, :] * k)
        return mag * jnp.cos(ei[:, None, :] * k), mag * jnp.sin(ei[:, None, :] * k)

    def cmul(xr, xi, yr, yi):
        return xr * yr - xi * yi, xr * yi + xi * yr

    a_re, a_im = a_pow([1.0])
    xr, xi = a_re[:, 0] - 1.0, a_im[:, 0]
    den = lam_re * lam_re + lam_im * lam_im
    qr, qi = (xr * lam_re + xi * lam_im) / den, (xi * lam_re - xr * lam_im) / den
    bt_re, bt_im = b_re.astype(F32).transpose(0, 2, 1), b_im.astype(F32).transpose(0, 2, 1)
    bb_re, bb_im = cmul(qr[:, None, :], qi[:, None, :], bt_re, bt_im)
    cc_re, cc_im = c_re.astype(F32), c_im.astype(F32)
    steps = jnp.arange(t, dtype=F32)
    half = t // 2

    def outer(pw, mat):
        r, i = cmul(pw[0][:, :, None, :], pw[1][:, :, None, :], mat[0][:, None], mat[1][:, None])
        return r.reshape(g, t * n, p), i.reshape(g, t * n, p)

    fb_re, fb_im = outer(a_pow(half - steps), (bb_re, bb_im))
    fc_re, fc_im = outer(a_pow(steps - half), (cc_re, cc_im))
    fb = jnp.concatenate([fb_re, -fb_im], axis=-1)
    fc = jnp.concatenate([fc_re, fc_im], axis=-1).transpose(0, 2, 1)
    v_re, v_im = outer(a_pow(t - 1.0 - steps), (bb_re, bb_im))
    v_mat = jnp.concatenate([v_re, v_im], axis=-1).astype(BF16)
    v_swap = jnp.concatenate([v_im, v_re], axis=-1).astype(BF16)
    w_re, w_im = outer(a_pow(steps + 1.0), (cc_re, cc_im))
    w_mat = jnp.concatenate([w_re, -w_im], axis=-1).transpose(0, 2, 1).astype(BF16)
    at_re, at_im = a_pow([float(t)])
    a1 = jnp.concatenate([at_re[:, 0], at_re[:, 0]], axis=-1)
    a2 = jnp.concatenate([-at_im[:, 0], at_im[:, 0]], axis=-1)
    d_row = jnp.tile(d_skip.astype(F32).reshape(g, n), (1, t))
    return fb, fc, v_mat, v_swap, w_mat, d_row, a1, a2


def _s5_core_body(u_ref, fb_ref, fc_ref, v_ref, vs_ref, w_ref, d_ref, a1_ref, a2_ref, z_ref,
                  perm_ref, m_ref, x_ref, loc_ref, locs_ref, prev_ref, s_ref, ss_ref, *, gb, cb, t, n):
    col, blk = pl.program_id(0), pl.program_id(1)
    width = t * n
    lanes = gb * n
    wide = t * lanes
    ln, lw = n.bit_length() - 1, width.bit_length() - 1

    @pl.when(jnp.logical_and(col == 0, blk == 0))
    def _():
        strip = 256
        c = lax.broadcasted_iota(jnp.int32, (strip, wide), 1)
        src = (((c >> ln) & (t - 1)) * lanes) + ((c >> lw) * n) + (c & (n - 1))
        for r0 in range(0, wide, strip):
            r = lax.broadcasted_iota(jnp.int32, (strip, wide), 0) + r0
            perm_ref[r0:r0 + strip, :] = jnp.where(r == src, 1.0, 0.0).astype(BF16)

    @pl.when(blk == 0)
    def _():
        rj = lax.broadcasted_iota(jnp.int32, (width, width), 0)
        ct = lax.broadcasted_iota(jnp.int32, (width, width), 1)
        for g in range(gb):
            fbh, fbl = _split2(fb_ref[g])
            fch, fcl = _split2(fc_ref[g])
            mm = (jnp.dot(fbh, fch, preferred_element_type=F32) + jnp.dot(fbh, fcl, preferred_element_type=F32)
                  + jnp.dot(fbl, fch, preferred_element_type=F32))
            mm = jnp.where((ct >> ln) >= (rj >> ln), mm, 0.0) + jnp.where(rj == ct, d_ref[pl.ds(g, 1), :], 0.0)
            m_ref[g] = mm.astype(BF16)
        s_ref[...] = jnp.zeros_like(s_ref)
        ss_ref[...] = jnp.zeros_like(ss_ref)

    x_ref[...] = u_ref[...].astype(F32)
    xcat = jnp.concatenate([x_ref[pl.ds(tt, cb, stride=t), :] for tt in range(t)], axis=1).astype(BF16)
    ug_all = jnp.dot(xcat, perm_ref[...], preferred_element_type=F32).astype(BF16)
    for g in range(gb):
        ug = ug_all[:, g * width:(g + 1) * width]
        loc_ref[g * cb:(g + 1) * cb, :] = jnp.dot(ug, v_ref[g], preferred_element_type=F32)
        locs_ref[g * cb:(g + 1) * cb, :] = jnp.dot(ug, vs_ref[g], preferred_element_type=F32)
    a1, a2 = a1_ref[...], a2_ref[...]

    def step(c, carry):
        s, ss = carry
        rows = pl.ds(c, gb, stride=cb)
        prev_ref[rows, :] = s
        s_new = a1 * s + a2 * ss + loc_ref[rows, :]
        ss_new = a1 * ss - a2 * s + locs_ref[rows, :]
        return s_new, ss_new

    s_ref[...], ss_ref[...] = lax.fori_loop(0, cb, step, (s_ref[...], ss_ref[...]), unroll=V7X_SUBLANES)
    zs = []
    for g in range(gb):
        ug = ug_all[:, g * width:(g + 1) * width]
        y = jnp.dot(ug, m_ref[g], preferred_element_type=F32)
        y += jnp.dot(prev_ref[g * cb:(g + 1) * cb, :].astype(BF16), w_ref[g], preferred_element_type=F32)
        zs.append(_gelu_tanh(y).astype(BF16))
    zcat = lax.dot_general(jnp.concatenate(zs, axis=1), perm_ref[...], (((1,), (1,)), ((), ())),
                           preferred_element_type=F32)
    for tt in range(t):
        x_ref[pl.ds(tt, cb, stride=t), :] = zcat[:, tt * lanes:(tt + 1) * lanes]
    z_ref[...] = x_ref[...].astype(z_ref.dtype)


def _s5_core(u, ops):
    fb, fc, v_mat, v_swap, w_mat, d_row, a1, a2 = ops
    seq, d = u.shape
    t, n = S5_CHUNK, S5_GROUP
    gb = V7X_LANES // n
    width, state = t * n, a1.shape[-1]
    chunks = seq // t
    cb = _pick(chunks, 512, V7X_SUBLANES)
    wide = t * V7X_LANES
    blk3 = lambda d1, d2: pl.BlockSpec((gb, d1, d2), lambda j, i: (j, 0, 0))
    blk2 = lambda d1: pl.BlockSpec((gb, d1), lambda j, i: (j, 0))
    return pl.pallas_call(
        functools.partial(_s5_core_body, gb=gb, cb=cb, t=t, n=n),
        grid=(d // V7X_LANES, chunks // cb),
        in_specs=[pl.BlockSpec((cb * t, V7X_LANES), lambda j, i: (i, j)),
                  blk3(width, state), blk3(state, width), blk3(width, state), blk3(width, state),
                  blk3(state, width), blk2(width), blk2(state), blk2(state)],
        out_specs=pl.BlockSpec((cb * t, V7X_LANES), lambda j, i: (i, j)),
        out_shape=jax.ShapeDtypeStruct((seq, d), BF16),
        scratch_shapes=[pltpu.VMEM((wide, wide), BF16),
                        pltpu.VMEM((gb, width, width), BF16),
                        pltpu.VMEM((cb * t, V7X_LANES), F32),
                        pltpu.VMEM((gb * cb, state), F32),
                        pltpu.VMEM((gb * cb, state), F32),
                        pltpu.VMEM((gb * cb, state), F32),
                        pltpu.VMEM((gb, state), F32),
                        pltpu.VMEM((gb, state), F32)],
        compiler_params=_params("arbitrary", "arbitrary"),
        name="s5_core",
    )(u, fb, fc, v_mat, v_swap, w_mat, d_row, a1, a2)


def _s5_mixer(xs, w_in, lam_re, lam_im, log_step, b_re, b_im, c_re, c_im, d_skip, w_glu, w_out):
    x, xb, r = xs
    ops = _s5_operators(lam_re, lam_im, log_step, b_re, b_im, c_re, c_im, d_skip)
    u = _matmul(xb, [w_in], [(r, "m")], _normed(_ep_identity), BF16, tm=1024, tn=1024, name="s5_in")
    z = _s5_core(u, ops)
    gated = _matmul(z, [w_glu], [(z, "mn")], _ep_glu, BF16, tm=1024, tn=1024, name="s5_glu")
    return _proj_residual(x, gated, w_out, "s5_out")


def _pool_band_body(u_ref, o_ref, hist_ref, *, tm, tn, hist, group_cols):
    j, i = pl.program_id(0), pl.program_id(1)

    @pl.when(i == 0)
    def _():
        hist_ref[...] = jnp.zeros_like(hist_ref)

    u = u_ref[...]
    ub = u.astype(BF16)
    ext = jnp.concatenate([hist_ref[...], ub], axis=0)
    gid = (j * tn) // group_cols
    win = jnp.int32(POOL_WINDOWS[0])
    for k, w in enumerate(POOL_WINDOWS[1:], start=1):
        win = jnp.where(gid == k, jnp.int32(w), win)
    lag = (lax.broadcasted_iota(jnp.int32, (tm, hist + tm), 0) + hist
           - lax.broadcasted_iota(jnp.int32, (tm, hist + tm), 1))
    band = jnp.where(lag >= 0, jnp.where(lag < win, 1.0, 0.0), 0.0).astype(BF16)
    wsum = jnp.dot(band, ext, preferred_element_type=F32)
    count = jnp.minimum(i * tm + lax.broadcasted_iota(jnp.int32, (tm, 1), 0) + 1, win).astype(F32)
    o_ref[...] = (wsum / count - u).astype(o_ref.dtype)
    hist_ref[...] = ub[tm - hist:, :]


def _pool_band(u, n_groups):
    seq, d = u.shape
    group_cols = d // n_groups
    tm = _pick(seq, 512)
    tn = _pick(group_cols, 512)
    hist = min(V7X_LANES, tm)
    assert hist >= max(POOL_WINDOWS)
    return pl.pallas_call(
        functools.partial(_pool_band_body, tm=tm, tn=tn, hist=hist, group_cols=group_cols),
        grid=(d // tn, seq // tm),
        in_specs=[pl.BlockSpec((tm, tn), lambda j, i: (i, j))],
        out_specs=pl.BlockSpec((tm, tn), lambda j, i: (i, j)),
        out_shape=jax.ShapeDtypeStruct((seq, d), BF16),
        scratch_shapes=[pltpu.VMEM((hist, tn), BF16)],
        compiler_params=_params("parallel", "arbitrary"),
        name="pool_band",
    )(u)


def _pool_group_body(a_ref, w_ref, s_ref, o_ref):
    o_ref[...] = (jnp.dot(a_ref[...], w_ref[...], preferred_element_type=F32) * s_ref[...]).astype(o_ref.dtype)


def _pool_group(diff, w_group, scale):
    seq, d = diff.shape
    n_groups, gc, _ = w_group.shape
    tm = _pick(seq, 1024)
    return pl.pallas_call(
        _pool_group_body,
        grid=(n_groups, seq // tm),
        in_specs=[pl.BlockSpec((tm, gc), lambda g, i: (i, g)),
                  pl.BlockSpec((None, gc, gc), lambda g, i: (g, 0, 0)),
                  pl.BlockSpec((1, gc), lambda g, i: (0, g))],
        out_specs=pl.BlockSpec((tm, gc), lambda g, i: (i, g)),
        out_shape=jax.ShapeDtypeStruct((seq, d), BF16),
        compiler_params=_params("parallel", "parallel"),
        name="pool_group",
    )(diff, w_group.astype(BF16), scale.reshape(1, d).astype(F32))


def _pool_mixer(xs, w_in, w_group, scale, w_out):
    x, xb, r = xs
    u = _matmul(xb, [w_in], [(r, "m")], _normed(_ep_identity), F32, tm=1024, tn=1024, name="pool_in")
    diff = _pool_band(u, len(POOL_WINDOWS))
    mixed = _pool_group(diff, w_group, scale)
    return _proj_residual(x, mixed, w_out, "pool_out")


def _forget_cumsum_body(fl_ref, b_ref, cum_ref, cumt_ref, carry_ref, *, tm):
    @pl.when(pl.program_id(0) == 0)
    def _():
        carry_ref[...] = jnp.zeros_like(carry_ref)

    x = fl_ref[...] + b_ref[...]
    logf = -(jnp.maximum(-x, 0.0) + jnp.log1p(jnp.exp(-jnp.abs(x))))
    tri = jnp.where(lax.broadcasted_iota(jnp.int32, (tm, tm), 0)
                    >= lax.broadcasted_iota(jnp.int32, (tm, tm), 1), 1.0, 0.0).astype(BF16)
    cs = carry_ref[...]
    for part in _split3(logf):
        cs = cs + jnp.dot(tri, part, preferred_element_type=F32)
    cum_ref[...] = cs * LOG2_E
    cumt_ref[...] = (cs * LOG2_E).T
    carry_ref[...] = cs[tm - 1:tm, :]


def _forget_cumsum(fl, b_row):
    seq, hp = fl.shape
    tm = _pick(seq, 256)
    return pl.pallas_call(
        functools.partial(_forget_cumsum_body, tm=tm),
        grid=(seq // tm,),
        in_specs=[pl.BlockSpec((tm, hp), lambda i: (i, 0)),
                  pl.BlockSpec((1, hp), lambda i: (0, 0))],
        out_specs=[pl.BlockSpec((tm, hp), lambda i: (i, 0)),
                   pl.BlockSpec((hp, tm), lambda i: (0, i))],
        out_shape=[jax.ShapeDtypeStruct((seq, hp), F32), jax.ShapeDtypeStruct((hp, seq), F32)],
        scratch_shapes=[pltpu.VMEM((1, hp), F32)],
        compiler_params=_params("arbitrary"),
        name="forget_cumsum",
    )(fl, b_row)


def _fox_attn_body(q_ref, k_ref, v_ref, cum_ref, cumt_ref, o_ref,
                   vaug_ref, s_ref, cq_ref, m_ref, acc_ref, *, tq, tk, rs):
    head, qi = pl.program_id(0), pl.program_id(1)
    lw, dh = V7X_LANES, FOX_HEAD_DIM

    @pl.when(qi == 0)
    def _():
        ones_col = jnp.where(lax.broadcasted_iota(jnp.int32, (tk, lw), 1) == 0, 1.0, 0.0).astype(BF16)
        for r0 in range(0, vaug_ref.shape[0], tk):
            vaug_ref[r0:r0 + tk, :dh] = v_ref[r0:r0 + tk, :]
            vaug_ref[r0:r0 + tk, dh:] = ones_col

    q = q_ref[...]
    lane = lax.broadcasted_iota(jnp.int32, cum_ref.shape, 1)
    cq = jnp.sum(jnp.where(lane == head, cum_ref[...], 0.0), axis=-1, keepdims=True)
    cq_ref[...] = jnp.broadcast_to(cq, (tq, lw))
    ck_row = head % V7X_SUBLANES
    m_ref[...] = jnp.full_like(m_ref, -jnp.inf)
    acc_ref[...] = jnp.zeros_like(acc_ref)

    def scores(b, slot, r0=0, nrows=tq):
        start = pl.multiple_of(b * tk, tk)
        ck = cumt_ref[pl.ds(ck_row, 1), pl.ds(start, tk)]
        s_ref[slot, r0:r0 + nrows, :] = lax.dot_general(
            q_ref[r0:r0 + nrows, :], k_ref[pl.ds(start, tk), :], (((1,), (1,)), ((), ())),
            preferred_element_type=F32) - ck

    def update(b, slot, masked, nxt=None):
        start = pl.multiple_of(b * tk, tk)
        vb = vaug_ref[pl.ds(start, tk), :]
        for r0 in range(0, tq, rs):
            if nxt is not None:
                scores(nxt, 1 - slot, r0, rs)
            rows = pl.ds(r0, rs)
            s = s_ref[slot, rows, :]
            if masked:
                qpos = qi * tq + r0 + lax.broadcasted_iota(jnp.int32, (rs, tk), 0)
                kpos = start + lax.broadcasted_iota(jnp.int32, (rs, tk), 1)
                s = jnp.where(qpos >= kpos, s, -jnp.inf)
            cq_rows = cq_ref[rows, :]
            m_prev = m_ref[rows, :]
            m_new = jnp.maximum(m_prev, jnp.max(s, axis=-1, keepdims=True) + cq_rows)
            alpha = jnp.exp2(m_prev - m_new)
            shift = m_new - cq_rows
            p = jnp.concatenate(
                [jnp.exp2(s[:, c * lw:(c + 1) * lw] - shift).astype(BF16) for c in range(tk // lw)], axis=1)
            pv = jnp.dot(p, vb, preferred_element_type=F32)
            for c0 in range(0, 2 * dh, lw):
                acc_ref[rows, c0:c0 + lw] = alpha * acc_ref[rows, c0:c0 + lw] + pv[:, c0:c0 + lw]
            m_ref[rows, :] = m_new

    first_diag = qi * (tq // tk)
    scores(0, 0)

    def pair(kk, carry):
        b = 2 * kk
        update(b, 0, False, nxt=b + 1)
        update(b + 1, 1, False, nxt=b + 2)
        return carry

    lax.fori_loop(0, first_diag // 2, pair, 0)
    update(first_diag, 0, True, nxt=first_diag + 1)
    update(first_diag + 1, 1, True)
    denom = jnp.sum(acc_ref[:, dh:], axis=-1, keepdims=True)
    o_ref[...] = (acc_ref[:, :dh] / denom).astype(o_ref.dtype)


def _fox_attention(qkv, cum, cumt, n_heads, tq):
    seq = qkv.shape[0]
    dh = FOX_HEAD_DIM
    hp = cum.shape[1]
    tk = tq // 2
    rs = _pick(tq, 512, V7X_SUBLANES)
    return pl.pallas_call(
        functools.partial(_fox_attn_body, tq=tq, tk=tk, rs=rs),
        grid=(n_heads, seq // tq),
        in_specs=[pl.BlockSpec((tq, dh), lambda h, i: (i, h)),
                  pl.BlockSpec((seq, dh), lambda h, i: (0, n_heads + h)),
                  pl.BlockSpec((seq, dh), lambda h, i: (0, 2 * n_heads + h)),
                  pl.BlockSpec((tq, hp), lambda h, i: (i, 0)),
                  pl.BlockSpec((V7X_SUBLANES, seq), lambda h, i: (h // V7X_SUBLANES, 0))],
        out_specs=pl.BlockSpec((tq, dh), lambda h, i: (i, h)),
        out_shape=jax.ShapeDtypeStruct((seq, n_heads * dh), BF16),
        scratch_shapes=[pltpu.VMEM((seq, 2 * dh), BF16),
                        pltpu.VMEM((2, tq, tk), F32),
                        pltpu.VMEM((tq, V7X_LANES), F32),
                        pltpu.VMEM((tq, V7X_LANES), F32),
                        pltpu.VMEM((tq, 2 * dh), F32)],
        compiler_params=_params("arbitrary", "arbitrary"),
        name="fox_attention",
    )(qkv, qkv, qkv, cum, cumt)


def _fox_mixer(xs, w_in, b_f, w_out, tq=1024):
    x, xb, r = xs
    seq, d = x.shape
    n_heads = d // FOX_HEAD_DIM
    hp = -(-n_heads // V7X_LANES) * V7X_LANES
    w_stack, layer = w_in
    w_f = jnp.pad(w_stack[layer][:, 3 * d:], ((0, 0), (0, hp - n_heads)))[None]
    b_row = jnp.pad(b_f.astype(F32), (0, hp - n_heads)).reshape(1, hp)
    sm_scale = FOX_HEAD_DIM ** -0.5 * LOG2_E
    col_scale = jnp.concatenate([jnp.full((1, d), sm_scale, F32), jnp.ones((1, 2 * d), F32)], axis=1)
    qkv = _matmul(xb, [w_in], [(col_scale, "n"), (r, "m")], _normed(_ep_colscale), BF16,
                  tm=1024, tn=1024, name="fox_qkv", n=3 * d)
    fl = _matmul(xb, [(w_f, 0)], [(r, "m")], _normed(_ep_identity), F32, tm=1024, tn=V7X_LANES,
                 name="fox_gate")
    cum, cumt = _forget_cumsum(fl, b_row)
    o = _fox_attention(qkv, cum, cumt, n_heads, _pick(seq, tq))
    return _proj_residual(x, o, w_out, "fox_out")


PLAIN_WEIGHTS = ("ffn1_w_down", "ffn2_w_down", "s5_w_glu", "s5_w_out", "pool_w_out", "fox_w_out")
NORMED_WEIGHTS = {"ffn1_w_gate": ("norm_ffn1", 0, 1), "ffn1_w_up": ("norm_ffn1", 0, 1),
                  "ffn2_w_gate": ("norm_ffn2", 0, 1), "ffn2_w_up": ("norm_ffn2", 0, 1),
                  "s5_w_in": ("norm_mix", 0, N_MIXERS), "pool_w_in": ("norm_mix", 1, N_MIXERS),
                  "fox_w_in": ("norm_mix", 2, N_MIXERS)}


def _prepare_weights(p):
    q = dict(p)
    for k in PLAIN_WEIGHTS:
        q[k] = p[k].astype(BF16)
    for k, (gain, first, stride) in NORMED_WEIGHTS.items():
        g = p[gain][first::stride][:p[k].shape[0]].astype(F32)
        q[k] = (g[:, :, None] * p[k]).astype(BF16)
    return q


def _trunk(x, p):
    depth = p["norm_ffn1"].shape[0]
    xs = (x,) + tuple(_row_stats(x))
    for i in range(depth):
        xs = _ffn(xs, (p["ffn1_w_gate"], i), (p["ffn1_w_up"], i), (p["ffn1_w_down"], i))
        j, kind = divmod(i, N_MIXERS)
        if kind == 0:
            xs = _s5_mixer(xs, (p["s5_w_in"], j), p["s5_lam_re"][j], p["s5_lam_im"][j], p["s5_log_step"][j],
                           p["s5_b_re"][j], p["s5_b_im"][j], p["s5_c_re"][j], p["s5_c_im"][j],
                           p["s5_d"][j], (p["s5_w_glu"], j), (p["s5_w_out"], j))
        elif kind == 1:
            xs = _pool_mixer(xs, (p["pool_w_in"], j), p["pool_w_group"][j], p["pool_scale"][j],
                             (p["pool_w_out"], j))
        else:
            xs = _fox_mixer(xs, (p["fox_w_in"], j), p["fox_b_f"][j], (p["fox_w_out"], j))
        xs = _ffn(xs, (p["ffn2_w_gate"], i), (p["ffn2_w_up"], i), (p["ffn2_w_down"], i))
    return _rmsnorm(xs[0], p["norm_final"], F32)


def kernel(x, norm_ffn1, ffn1_w_gate, ffn1_w_up, ffn1_w_down, norm_mix, s5_w_in, s5_lam_re, s5_lam_im, s5_log_step, s5_b_re, s5_b_im, s5_c_re, s5_c_im, s5_d, s5_w_glu, s5_w_out, pool_w_in, pool_w_group, pool_scale, pool_w_out, fox_w_in, fox_b_f, fox_w_out, norm_ffn2, ffn2_w_gate, ffn2_w_up, ffn2_w_down, norm_final):
    p = dict(norm_ffn1=norm_ffn1, ffn1_w_gate=ffn1_w_gate, ffn1_w_up=ffn1_w_up, ffn1_w_down=ffn1_w_down,
             norm_mix=norm_mix, s5_w_in=s5_w_in, s5_lam_re=s5_lam_re, s5_lam_im=s5_lam_im,
             s5_log_step=s5_log_step, s5_b_re=s5_b_re, s5_b_im=s5_b_im, s5_c_re=s5_c_re, s5_c_im=s5_c_im,
             s5_d=s5_d, s5_w_glu=s5_w_glu, s5_w_out=s5_w_out, pool_w_in=pool_w_in, pool_w_group=pool_w_group,
             pool_scale=pool_scale, pool_w_out=pool_w_out, fox_w_in=fox_w_in, fox_b_f=fox_b_f,
             fox_w_out=fox_w_out, norm_ffn2=norm_ffn2, ffn2_w_gate=ffn2_w_gate, ffn2_w_up=ffn2_w_up,
             ffn2_w_down=ffn2_w_down, norm_final=norm_final)
    p = _prepare_weights(p)
    return jnp.stack([_trunk(x[b], p) for b in range(x.shape[0])])
```

```python
import functools
import math

import jax
import jax.numpy as jnp
from jax import lax
from jax.experimental import pallas as pl
from jax.experimental.pallas import tpu as pltpu

RMS_EPS = 1e-6
FFN_RES_WEIGHT = 0.5
N_MIXERS = 3
S5_GROUP = 16
S5_CHUNK = 16
POOL_WINDOWS = (2, 4, 8, 16)
FOX_HEAD_DIM = 128
LOG2_E = 1.0 / math.log(2.0)

V7X_LANES = 128
V7X_SUBLANES = 8
V7X_VMEM_BYTES = 64 * 2**20
VMEM_LIMIT = V7X_VMEM_BYTES - 8 * 2**20

F32 = jnp.float32
BF16 = jnp.bfloat16


def _pick(dim, pref, align=V7X_LANES):
    if dim <= pref:
        return dim
    t = (pref // align) * align
    while t >= align:
        if dim % t == 0:
            return t
        t -= align
    return dim


def _params(*semantics):
    return pltpu.CompilerParams(dimension_semantics=semantics, vmem_limit_bytes=VMEM_LIMIT)


def _split2(x):
    hi = x.astype(BF16)
    return hi, (x - hi.astype(F32)).astype(BF16)


def _split3(x):
    hi = x.astype(BF16)
    r1 = x - hi.astype(F32)
    mid = r1.astype(BF16)
    return hi, mid, (r1 - mid.astype(F32)).astype(BF16)


def _rmsnorm_body(x_ref, g_ref, o_ref):
    x = x_ref[...]
    ms = jnp.mean(x * x, axis=-1, keepdims=True)
    o_ref[...] = (x * lax.rsqrt(ms + RMS_EPS) * g_ref[...]).astype(o_ref.dtype)


def _rmsnorm(x, g, out_dtype):
    rows, d = x.shape
    tr = _pick(rows, 256)
    return pl.pallas_call(
        _rmsnorm_body,
        grid=(rows // tr,),
        in_specs=[pl.BlockSpec((tr, d), lambda i: (i, 0)),
                  pl.BlockSpec((1, d), lambda i: (0, 0))],
        out_specs=pl.BlockSpec((tr, d), lambda i: (i, 0)),
        out_shape=jax.ShapeDtypeStruct((rows, d), out_dtype),
        compiler_params=_params("parallel"),
        name="rmsnorm",
    )(x, g.reshape(1, d).astype(F32))


def _mm_body(*refs, n_b, kinds, sub, epilogue, stats):
    a_ref = refs[0]
    b_refs = refs[1:1 + n_b]
    e_refs = refs[1 + n_b:1 + n_b + len(kinds)]
    o_ref = refs[1 + n_b + len(kinds)]
    a = a_ref[...]
    ssq = None
    for c0 in range(0, o_ref.shape[1], sub):
        cols = slice(c0, c0 + sub)
        parts = [jnp.dot(a, b[:, cols], preferred_element_type=F32) for b in b_refs]
        tiles = [e[...] if kind == "m" else e[:, cols] for e, kind in zip(e_refs, kinds)]
        val = epilogue(parts, tiles[:-1] if stats else tiles)
        o_ref[:, cols] = val.astype(o_ref.dtype)
        if stats:
            xb_ref, r_ref, ssq_ref = refs[2 + n_b + len(kinds):]
            xb_ref[:, cols] = (val * tiles[-1]).astype(BF16)
            part = jnp.sum(val * val, axis=-1, keepdims=True)
            ssq = part if ssq is None else ssq + part
    if stats:
        j, nj = pl.program_id(1), pl.num_programs(1)

        @pl.when(j == 0)
        def _():
            ssq_ref[...] = ssq

        @pl.when(j > 0)
        def _():
            ssq_ref[...] += ssq

        @pl.when(j == nj - 1)
        def _():
            r_ref[...] = lax.rsqrt(ssq_ref[...] / stats + RMS_EPS)


def _matmul(a, bs, extras, epilogue, out_dtype, *, tm, tn, name, n=None, stats=False):
    m, kdim = a.shape
    n = bs[0][0].shape[2] if n is None else n
    tm, tn = _pick(m, tm), _pick(n, tn)
    sub = _pick(tn, 2 * V7X_LANES)
    in_specs = [pl.BlockSpec((tm, kdim), lambda i, j: (i, 0))]
    for _, layer in bs:
        in_specs.append(pl.BlockSpec((None, kdim, tn), lambda i, j, layer=layer: (layer, 0, j)))
    for _, kind in extras:
        if kind == "mn":
            in_specs.append(pl.BlockSpec((tm, tn), lambda i, j: (i, j)))
        elif kind == "n":
            in_specs.append(pl.BlockSpec((1, tn), lambda i, j: (0, j)))
        else:
            in_specs.append(pl.BlockSpec((tm, 1), lambda i, j: (i, 0)))
    out_specs = pl.BlockSpec((tm, tn), lambda i, j: (i, j))
    out_shape = jax.ShapeDtypeStruct((m, n), out_dtype)
    scratch = []
    if stats:
        out_specs = [out_specs, pl.BlockSpec((tm, tn), lambda i, j: (i, j)), pl.BlockSpec((tm, 1), lambda i, j: (i, 0))]
        out_shape = [out_shape, jax.ShapeDtypeStruct((m, n), BF16), jax.ShapeDtypeStruct((m, 1), F32)]
        scratch = [pltpu.VMEM((tm, 1), F32)]
    return pl.pallas_call(
        functools.partial(_mm_body, n_b=len(bs), kinds=tuple(k for _, k in extras), sub=sub,
                          epilogue=epilogue, stats=float(n) if stats else 0.0),
        grid=(m // tm, n // tn),
        in_specs=in_specs,
        out_specs=out_specs,
        out_shape=out_shape,
        scratch_shapes=scratch,
        compiler_params=_params("parallel", "arbitrary"),
        name=name,
    )(a, *[w for w, _ in bs], *[e for e, _ in extras])


def _ep_identity(parts, extras):
    return parts[0]


def _ep_colscale(parts, extras):
    return parts[0] * extras[0]


def _ep_swiglu(parts, extras):
    gate, up = parts
    return gate * (1.0 / (1.0 + jnp.exp(-gate))) * up


def _ep_residual(weight, parts, extras):
    return extras[0] + weight * parts[0]


def _ep_glu(parts, extras):
    z = extras[0].astype(F32)
    return z * (1.0 / (1.0 + jnp.exp(-parts[0])))


def _gelu_tanh(x):
    c = math.sqrt(2.0 / math.pi)
    return x * (0.5 * (1.0 + jnp.tanh(c * (x + 0.044715 * (x * x * x)))))


def _normed(epilogue):
    def scaled(parts, extras):
        return epilogue([extras[-1] * p for p in parts], extras[:-1])
    return scaled


def _row_stats_body(x_ref, g_ref, xb_ref, r_ref):
    x = x_ref[...]
    xb_ref[...] = (x * g_ref[...]).astype(BF16)
    r_ref[...] = lax.rsqrt(jnp.mean(x * x, axis=-1, keepdims=True) + RMS_EPS)


def _gain_row(g):
    return g.reshape(1, -1).astype(F32)


def _row_stats(x, gain):
    rows, d = x.shape
    tr = _pick(rows, 256)
    return pl.pallas_call(
        _row_stats_body,
        grid=(rows // tr,),
        in_specs=[pl.BlockSpec((tr, d), lambda i: (i, 0)), pl.BlockSpec((1, d), lambda i: (0, 0))],
        out_specs=[pl.BlockSpec((tr, d), lambda i: (i, 0)), pl.BlockSpec((tr, 1), lambda i: (i, 0))],
        out_shape=[jax.ShapeDtypeStruct((rows, d), BF16), jax.ShapeDtypeStruct((rows, 1), F32)],
        compiler_params=_params("parallel"),
        name="row_stats",
    )(x, _gain_row(gain))


def _ffn(xs, w_gate, w_up, w_down, next_gain):
    x, xb, r = xs
    act = _matmul(xb, [w_gate, w_up], [(r, "m")], _normed(_ep_swiglu), BF16, tm=1024, tn=512, name="ffn_up")
    return _matmul(act, [w_down], [(x, "mn"), (_gain_row(next_gain), "n")],
                   functools.partial(_ep_residual, FFN_RES_WEIGHT), F32,
                   tm=1024, tn=512, name="ffn_down", stats=True)


def _proj_residual(x, a, w, name, next_gain):
    return _matmul(a, [w], [(x, "mn"), (_gain_row(next_gain), "n")], functools.partial(_ep_residual, 1.0), F32,
                   tm=1024, tn=512, name=name, stats=True)


def _s5_operators(lam_re, lam_im, log_step, b_re, b_im, c_re, c_im, d_skip):
    g, p = lam_re.shape
    n, t = S5_GROUP, S5_CHUNK
    lam_re, lam_im = lam_re.astype(F32), lam_im.astype(F32)
    step = jnp.exp(log_step.astype(F32))[:, None]
    er, ei = lam_re * step, lam_im * step

    def a_pow(k):
        k = jnp.asarray(k, F32)[None, :, None]
        mag = jnp.exp(er[:, None, :] * k)
        return mag * jnp.cos(ei[:, None, :] * k), mag * jnp.sin(ei[:, None, :] * k)

    def cmul(xr, xi, yr, yi):
        return xr * yr - xi * yi, xr * yi + xi * yr

    a_re, a_im = a_pow([1.0])
    xr, xi = a_re[:, 0] - 1.0, a_im[:, 0]
    den = lam_re * lam_re + lam_im * lam_im
    qr, qi = (xr * lam_re + xi * lam_im) / den, (xi * lam_re - xr * lam_im) / den
    bt_re, bt_im = b_re.astype(F32).transpose(0, 2, 1), b_im.astype(F32).transpose(0, 2, 1)
    bb_re, bb_im = cmul(qr[:, None, :], qi[:, None, :], bt_re, bt_im)
    cc_re, cc_im = c_re.astype(F32), c_im.astype(F32)
    steps = jnp.arange(t, dtype=F32)
    half = t // 2

    def outer(pw, mat):
        r, i = cmul(pw[0][:, :, None, :], pw[1][:, :, None, :], mat[0][:, None], mat[1][:, None])
        return r.reshape(g, t * n, p), i.reshape(g, t * n, p)

    fb_re, fb_im = outer(a_pow(half - steps), (bb_re, bb_im))
    fc_re, fc_im = outer(a_pow(steps - half), (cc_re, cc_im))
    fb = jnp.concatenate([fb_re, -fb_im], axis=-1)
    fc = jnp.concatenate([fc_re, fc_im], axis=-1).transpose(0, 2, 1)
    v_re, v_im = outer(a_pow(t - 1.0 - steps), (bb_re, bb_im))
    v_mat = jnp.concatenate([v_re, v_im], axis=-1).astype(BF16)
    v_swap = jnp.concatenate([v_im, v_re], axis=-1).astype(BF16)
    w_re, w_im = outer(a_pow(steps + 1.0), (cc_re, cc_im))
    w_mat = jnp.concatenate([w_re, -w_im], axis=-1).transpose(0, 2, 1).astype(BF16)
    at_re, at_im = a_pow([float(t)])
    a1 = jnp.concatenate([at_re[:, 0], at_re[:, 0]], axis=-1)
    a2 = jnp.concatenate([-at_im[:, 0], at_im[:, 0]], axis=-1)
    d_row = jnp.tile(d_skip.astype(F32).reshape(g, n), (1, t))
    return fb, fc, v_mat, v_swap, w_mat, d_row, a1, a2


def _s5_core_body(u_ref, fb_ref, fc_ref, v_ref, vs_ref, w_ref, d_ref, a1_ref, a2_ref, z_ref,
                  perm_ref, m_ref, x_ref, loc_ref, locs_ref, prev_ref, s_ref, ss_ref, *, gb, cb, t, n):
    col, blk = pl.program_id(0), pl.program_id(1)
    width = t * n
    lanes = gb * n
    wide = t * lanes
    ln, lw = n.bit_length() - 1, width.bit_length() - 1

    @pl.when(jnp.logical_and(col == 0, blk == 0))
    def _():
        strip = 256
        c = lax.broadcasted_iota(jnp.int32, (strip, wide), 1)
        src = (((c >> ln) & (t - 1)) * lanes) + ((c >> lw) * n) + (c & (n - 1))
        for r0 in range(0, wide, strip):
            r = lax.broadcasted_iota(jnp.int32, (strip, wide), 0) + r0
            perm_ref[r0:r0 + strip, :] = jnp.where(r == src, 1.0, 0.0).astype(BF16)

    @pl.when(blk == 0)
    def _():
        rj = lax.broadcasted_iota(jnp.int32, (width, width), 0)
        ct = lax.broadcasted_iota(jnp.int32, (width, width), 1)
        for g in range(gb):
            fbh, fbl = _split2(fb_ref[g])
            fch, fcl = _split2(fc_ref[g])
            mm = (jnp.dot(fbh, fch, preferred_element_type=F32) + jnp.dot(fbh, fcl, preferred_element_type=F32)
                  + jnp.dot(fbl, fch, preferred_element_type=F32))
            mm = jnp.where((ct >> ln) >= (rj >> ln), mm, 0.0) + jnp.where(rj == ct, d_ref[pl.ds(g, 1), :], 0.0)
            m_ref[g] = mm.astype(BF16)
        s_ref[...] = jnp.zeros_like(s_ref)
        ss_ref[...] = jnp.zeros_like(ss_ref)

    x_ref[...] = u_ref[...].astype(F32)
    xcat = jnp.concatenate([x_ref[pl.ds(tt, cb, stride=t), :] for tt in range(t)], axis=1).astype(BF16)
    ug_all = jnp.dot(xcat, perm_ref[...], preferred_element_type=F32).astype(BF16)
    for g in range(gb):
        ug = ug_all[:, g * width:(g + 1) * width]
        loc_ref[g * cb:(g + 1) * cb, :] = jnp.dot(ug, v_ref[g], preferred_element_type=F32)
        locs_ref[g * cb:(g + 1) * cb, :] = jnp.dot(ug, vs_ref[g], preferred_element_type=F32)
    a1, a2 = a1_ref[...], a2_ref[...]

    def step(c, carry):
        s, ss = carry
        rows = pl.ds(c, gb, stride=cb)
        prev_ref[rows, :] = s
        s_new = a1 * s + a2 * ss + loc_ref[rows, :]
        ss_new = a1 * ss - a2 * s + locs_ref[rows, :]
        return s_new, ss_new

    s_ref[...], ss_ref[...] = lax.fori_loop(0, cb, step, (s_ref[...], ss_ref[...]), unroll=V7X_SUBLANES)
    zs = []
    for g in range(gb):
        ug = ug_all[:, g * width:(g + 1) * width]
        y = jnp.dot(ug, m_ref[g], preferred_element_type=F32)
        y += jnp.dot(prev_ref[g * cb:(g + 1) * cb, :].astype(BF16), w_ref[g], preferred_element_type=F32)
        zs.append(_gelu_tanh(y).astype(BF16))
    zcat = lax.dot_general(jnp.concatenate(zs, axis=1), perm_ref[...], (((1,), (1,)), ((), ())),
                           preferred_element_type=F32)
    for tt in range(t):
        x_ref[pl.ds(tt, cb, stride=t), :] = zcat[:, tt * lanes:(tt + 1) * lanes]
    z_ref[...] = x_ref[...].astype(z_ref.dtype)


def _s5_core(u, ops):
    fb, fc, v_mat, v_swap, w_mat, d_row, a1, a2 = ops
    seq, d = u.shape
    t, n = S5_CHUNK, S5_GROUP
    gb = V7X_LANES // n
    width, state = t * n, a1.shape[-1]
    chunks = seq // t
    cb = _pick(chunks, 512, V7X_SUBLANES)
    wide = t * V7X_LANES
    blk3 = lambda d1, d2: pl.BlockSpec((gb, d1, d2), lambda j, i: (j, 0, 0))
    blk2 = lambda d1: pl.BlockSpec((gb, d1), lambda j, i: (j, 0))
    return pl.pallas_call(
        functools.partial(_s5_core_body, gb=gb, cb=cb, t=t, n=n),
        grid=(d // V7X_LANES, chunks // cb),
        in_specs=[pl.BlockSpec((cb * t, V7X_LANES), lambda j, i: (i, j)),
                  blk3(width, state), blk3(state, width), blk3(width, state), blk3(width, state),
                  blk3(state, width), blk2(width), blk2(state), blk2(state)],
        out_specs=pl.BlockSpec((cb * t, V7X_LANES), lambda j, i: (i, j)),
        out_shape=jax.ShapeDtypeStruct((seq, d), BF16),
        scratch_shapes=[pltpu.VMEM((wide, wide), BF16),
                        pltpu.VMEM((gb, width, width), BF16),
                        pltpu.VMEM((cb * t, V7X_LANES), F32),
                        pltpu.VMEM((gb * cb, state), F32),
                        pltpu.VMEM((gb * cb, state), F32),
                        pltpu.VMEM((gb * cb, state), F32),
                        pltpu.VMEM((gb, state), F32),
                        pltpu.VMEM((gb, state), F32)],
        compiler_params=_params("arbitrary", "arbitrary"),
        name="s5_core",
    )(u, fb, fc, v_mat, v_swap, w_mat, d_row, a1, a2)


def _s5_mixer(xs, w_in, lam_re, lam_im, log_step, b_re, b_im, c_re, c_im, d_skip, w_glu, w_out, next_gain):
    x, xb, r = xs
    ops = _s5_operators(lam_re, lam_im, log_step, b_re, b_im, c_re, c_im, d_skip)
    u = _matmul(xb, [w_in], [(r, "m")], _normed(_ep_identity), BF16, tm=1024, tn=1024, name="s5_in")
    z = _s5_core(u, ops)
    gated = _matmul(z, [w_glu], [(z, "mn")], _ep_glu, BF16, tm=1024, tn=1024, name="s5_glu")
    return _proj_residual(x, gated, w_out, "s5_out", next_gain)


def _pool_band_body(u_ref, o_ref, hist_ref, *, tm, tn, hist, group_cols):
    j, i = pl.program_id(0), pl.program_id(1)

    @pl.when(i == 0)
    def _():
        hist_ref[...] = jnp.zeros_like(hist_ref)

    u = u_ref[...]
    ub = u.astype(BF16)
    ext = jnp.concatenate([hist_ref[...], ub], axis=0)
    gid = (j * tn) // group_cols
    win = jnp.int32(POOL_WINDOWS[0])
    for k, w in enumerate(POOL_WINDOWS[1:], start=1):
        win = jnp.where(gid == k, jnp.int32(w), win)
    lag = (lax.broadcasted_iota(jnp.int32, (tm, hist + tm), 0) + hist
           - lax.broadcasted_iota(jnp.int32, (tm, hist + tm), 1))
    band = jnp.where(lag >= 0, jnp.where(lag < win, 1.0, 0.0), 0.0).astype(BF16)
    wsum = jnp.dot(band, ext, preferred_element_type=F32)
    count = jnp.minimum(i * tm + lax.broadcasted_iota(jnp.int32, (tm, 1), 0) + 1, win).astype(F32)
    o_ref[...] = (wsum / count - u).astype(o_ref.dtype)
    hist_ref[...] = ub[tm - hist:, :]


def _pool_band(u, n_groups):
    seq, d = u.shape
    group_cols = d // n_groups
    tm = _pick(seq, 512)
    tn = _pick(group_cols, 512)
    hist = min(V7X_LANES, tm)
    assert hist >= max(POOL_WINDOWS)
    return pl.pallas_call(
        functools.partial(_pool_band_body, tm=tm, tn=tn, hist=hist, group_cols=group_cols),
        grid=(d // tn, seq // tm),
        in_specs=[pl.BlockSpec((tm, tn), lambda j, i: (i, j))],
        out_specs=pl.BlockSpec((tm, tn), lambda j, i: (i, j)),
        out_shape=jax.ShapeDtypeStruct((seq, d), BF16),
        scratch_shapes=[pltpu.VMEM((hist, tn), BF16)],
        compiler_params=_params("parallel", "arbitrary"),
        name="pool_band",
    )(u)


def _pool_group_body(a_ref, w_ref, s_ref, o_ref):
    o_ref[...] = (jnp.dot(a_ref[...], w_ref[...], preferred_element_type=F32) * s_ref[...]).astype(o_ref.dtype)


def _pool_group(diff, w_group, scale):
    seq, d = diff.shape
    n_groups, gc, _ = w_group.shape
    tm = _pick(seq, 1024)
    return pl.pallas_call(
        _pool_group_body,
        grid=(n_groups, seq // tm),
        in_specs=[pl.BlockSpec((tm, gc), lambda g, i: (i, g)),
                  pl.BlockSpec((None, gc, gc), lambda g, i: (g, 0, 0)),
                  pl.BlockSpec((1, gc), lambda g, i: (0, g))],
        out_specs=pl.BlockSpec((tm, gc), lambda g, i: (i, g)),
        out_shape=jax.ShapeDtypeStruct((seq, d), BF16),
        compiler_params=_params("parallel", "parallel"),
        name="pool_group",
    )(diff, w_group.astype(BF16), scale.reshape(1, d).astype(F32))


def _pool_mixer(xs, w_in, w_group, scale, w_out, next_gain):
    x, xb, r = xs
    u = _matmul(xb, [w_in], [(r, "m")], _normed(_ep_identity), F32, tm=1024, tn=1024, name="pool_in")
    diff = _pool_band(u, len(POOL_WINDOWS))
    mixed = _pool_group(diff, w_group, scale)
    return _proj_residual(x, mixed, w_out, "pool_out", next_gain)


def _forget_cumsum_body(fl_ref, b_ref, cum_ref, cumt_ref, carry_ref, *, tm):
    @pl.when(pl.program_id(0) == 0)
    def _():
        carry_ref[...] = jnp.zeros_like(carry_ref)

    x = fl_ref[...] + b_ref[...]
    logf = -(jnp.maximum(-x, 0.0) + jnp.log1p(jnp.exp(-jnp.abs(x))))
    tri = jnp.where(lax.broadcasted_iota(jnp.int32, (tm, tm), 0)
                    >= lax.broadcasted_iota(jnp.int32, (tm, tm), 1), 1.0, 0.0).astype(BF16)
    cs = carry_ref[...]
    for part in _split3(logf):
        cs = cs + jnp.dot(tri, part, preferred_element_type=F32)
    cum_ref[...] = cs * LOG2_E
    cumt_ref[...] = (cs * LOG2_E).T
    carry_ref[...] = cs[tm - 1:tm, :]


def _forget_cumsum(fl, b_row):
    seq, hp = fl.shape
    tm = _pick(seq, 256)
    return pl.pallas_call(
        functools.partial(_forget_cumsum_body, tm=tm),
        grid=(seq // tm,),
        in_specs=[pl.BlockSpec((tm, hp), lambda i: (i, 0)),
                  pl.BlockSpec((1, hp), lambda i: (0, 0))],
        out_specs=[pl.BlockSpec((tm, hp), lambda i: (i, 0)),
                   pl.BlockSpec((hp, tm), lambda i: (0, i))],
        out_shape=[jax.ShapeDtypeStruct((seq, hp), F32), jax.ShapeDtypeStruct((hp, seq), F32)],
        scratch_shapes=[pltpu.VMEM((1, hp), F32)],
        compiler_params=_params("arbitrary"),
        name="forget_cumsum",
    )(fl, b_row)


def _fox_attn_body(q_ref, k_ref, v_ref, cum_ref, cumt_ref, o_ref,
                   vaug_ref, s_ref, cq_ref, m_ref, acc_ref, *, tq, tk, rs):
    head, qi = pl.program_id(0), pl.program_id(1)
    lw, dh = V7X_LANES, FOX_HEAD_DIM

    @pl.when(qi == 0)
    def _():
        ones_col = jnp.where(lax.broadcasted_iota(jnp.int32, (tk, lw), 1) == 0, 1.0, 0.0).astype(BF16)
        for r0 in range(0, vaug_ref.shape[0], tk):
            vaug_ref[r0:r0 + tk, :dh] = v_ref[r0:r0 + tk, :]
            vaug_ref[r0:r0 + tk, dh:] = ones_col

    q = q_ref[...]
    lane = lax.broadcasted_iota(jnp.int32, cum_ref.shape, 1)
    cq = jnp.sum(jnp.where(lane == head, cum_ref[...], 0.0), axis=-1, keepdims=True)
    cq_ref[...] = jnp.broadcast_to(cq, (tq, lw))
    ck_row = head % V7X_SUBLANES
    m_ref[...] = jnp.full_like(m_ref, -jnp.inf)
    acc_ref[...] = jnp.zeros_like(acc_ref)

    def scores(b, slot, r0=0, nrows=tq):
        start = pl.multiple_of(b * tk, tk)
        ck = cumt_ref[pl.ds(ck_row, 1), pl.ds(start, tk)]
        s_ref[slot, r0:r0 + nrows, :] = lax.dot_general(
            q_ref[r0:r0 + nrows, :], k_ref[pl.ds(start, tk), :], (((1,), (1,)), ((), ())),
            preferred_element_type=F32) - ck

    def update(b, slot, masked, nxt=None):
        start = pl.multiple_of(b * tk, tk)
        vb = vaug_ref[pl.ds(start, tk), :]
        for r0 in range(0, tq, rs):
            if nxt is not None:
                scores(nxt, 1 - slot, r0, rs)
            rows = pl.ds(r0, rs)
            s = s_ref[slot, rows, :]
            if masked:
                qpos = qi * tq + r0 + lax.broadcasted_iota(jnp.int32, (rs, tk), 0)
                kpos = start + lax.broadcasted_iota(jnp.int32, (rs, tk), 1)
                s = jnp.where(qpos >= kpos, s, -jnp.inf)
            cq_rows = cq_ref[rows, :]
            m_prev = m_ref[rows, :]
            m_new = jnp.maximum(m_prev, jnp.max(s, axis=-1, keepdims=True) + cq_rows)
            alpha = jnp.exp2(m_prev - m_new)
            shift = m_new - cq_rows
            p = jnp.concatenate(
                [jnp.exp2(s[:, c * lw:(c + 1) * lw] - shift).astype(BF16) for c in range(tk // lw)], axis=1)
            pv = jnp.dot(p, vb, preferred_element_type=F32)
            for c0 in range(0, 2 * dh, lw):
                acc_ref[rows, c0:c0 + lw] = alpha * acc_ref[rows, c0:c0 + lw] + pv[:, c0:c0 + lw]
            m_ref[rows, :] = m_new

    first_diag = qi * (tq // tk)
    scores(0, 0)

    def pair(kk, carry):
        b = 2 * kk
        update(b, 0, False, nxt=b + 1)
        update(b + 1, 1, False, nxt=b + 2)
        return carry

    lax.fori_loop(0, first_diag // 2, pair, 0)
    update(first_diag, 0, True, nxt=first_diag + 1)
    update(first_diag + 1, 1, True)
    denom = jnp.sum(acc_ref[:, dh:], axis=-1, keepdims=True)
    o_ref[...] = (acc_ref[:, :dh] / denom).astype(o_ref.dtype)


def _fox_attention(qkv, cum, cumt, n_heads, tq):
    seq = qkv.shape[0]
    dh = FOX_HEAD_DIM
    hp = cum.shape[1]
    tk = tq // 2
    rs = _pick(tq, 512, V7X_SUBLANES)
    return pl.pallas_call(
        functools.partial(_fox_attn_body, tq=tq, tk=tk, rs=rs),
        grid=(n_heads, seq // tq),
        in_specs=[pl.BlockSpec((tq, dh), lambda h, i: (i, h)),
                  pl.BlockSpec((seq, dh), lambda h, i: (0, n_heads + h)),
                  pl.BlockSpec((seq, dh), lambda h, i: (0, 2 * n_heads + h)),
                  pl.BlockSpec((tq, hp), lambda h, i: (i, 0)),
                  pl.BlockSpec((V7X_SUBLANES, seq), lambda h, i: (h // V7X_SUBLANES, 0))],
        out_specs=pl.BlockSpec((tq, dh), lambda h, i: (i, h)),
        out_shape=jax.ShapeDtypeStruct((seq, n_heads * dh), BF16),
        scratch_shapes=[pltpu.VMEM((seq, 2 * dh), BF16),
                        pltpu.VMEM((2, tq, tk), F32),
                        pltpu.VMEM((tq, V7X_LANES), F32),
                        pltpu.VMEM((tq, V7X_LANES), F32),
                        pltpu.VMEM((tq, 2 * dh), F32)],
        compiler_params=_params("arbitrary", "arbitrary"),
        name="fox_attention",
    )(qkv, qkv, qkv, cum, cumt)


def _fox_mixer(xs, w_in, b_f, w_out, next_gain, tq=1024):
    x, xb, r = xs
    seq, d = x.shape
    n_heads = d // FOX_HEAD_DIM
    hp = -(-n_heads // V7X_LANES) * V7X_LANES
    w_stack, layer = w_in
    w_f = jnp.pad(w_stack[layer][:, 3 * d:], ((0, 0), (0, hp - n_heads)))[None]
    b_row = jnp.pad(b_f.astype(F32), (0, hp - n_heads)).reshape(1, hp)
    sm_scale = FOX_HEAD_DIM ** -0.5 * LOG2_E
    col_scale = jnp.concatenate([jnp.full((1, d), sm_scale, F32), jnp.ones((1, 2 * d), F32)], axis=1)
    qkv = _matmul(xb, [w_in], [(col_scale, "n"), (r, "m")], _normed(_ep_colscale), BF16,
                  tm=1024, tn=1024, name="fox_qkv", n=3 * d)
    fl = _matmul(xb, [(w_f, 0)], [(r, "m")], _normed(_ep_identity), F32, tm=1024, tn=V7X_LANES,
                 name="fox_gate")
    cum, cumt = _forget_cumsum(fl, b_row)
    o = _fox_attention(qkv, cum, cumt, n_heads, _pick(seq, tq))
    return _proj_residual(x, o, w_out, "fox_out", next_gain)


MATMUL_WEIGHTS = ("ffn1_w_gate", "ffn1_w_up", "ffn1_w_down", "ffn2_w_gate", "ffn2_w_up", "ffn2_w_down",
                  "s5_w_in", "s5_w_glu", "s5_w_out", "pool_w_in", "pool_w_out", "fox_w_in", "fox_w_out")


def _trunk(x, p):
    depth = p["norm_ffn1"].shape[0]
    xs = (x,) + tuple(_row_stats(x, p["norm_ffn1"][0]))
    for i in range(depth):
        xs = _ffn(xs, (p["ffn1_w_gate"], i), (p["ffn1_w_up"], i), (p["ffn1_w_down"], i), p["norm_mix"][i])
        j, kind = divmod(i, N_MIXERS)
        g2 = p["norm_ffn2"][i]
        if kind == 0:
            xs = _s5_mixer(xs, (p["s5_w_in"], j), p["s5_lam_re"][j], p["s5_lam_im"][j], p["s5_log_step"][j],
                           p["s5_b_re"][j], p["s5_b_im"][j], p["s5_c_re"][j], p["s5_c_im"][j],
                           p["s5_d"][j], (p["s5_w_glu"], j), (p["s5_w_out"], j), g2)
        elif kind == 1:
            xs = _pool_mixer(xs, (p["pool_w_in"], j), p["pool_w_group"][j], p["pool_scale"][j],
                             (p["pool_w_out"], j), g2)
        else:
            xs = _fox_mixer(xs, (p["fox_w_in"], j), p["fox_b_f"][j], (p["fox_w_out"], j), g2)
        g_next = p["norm_ffn1"][i + 1] if i + 1 < depth else p["norm_final"]
        xs = _ffn(xs, (p["ffn2_w_gate"], i), (p["ffn2_w_up"], i), (p["ffn2_w_down"], i), g_next)
    return _rmsnorm(xs[0], p["norm_final"], F32)


def kernel(x, norm_ffn1, ffn1_w_gate, ffn1_w_up, ffn1_w_down, norm_mix, s5_w_in, s5_lam_re, s5_lam_im, s5_log_step, s5_b_re, s5_b_im, s5_c_re, s5_c_im, s5_d, s5_w_glu, s5_w_out, pool_w_in, pool_w_group, pool_scale, pool_w_out, fox_w_in, fox_b_f, fox_w_out, norm_ffn2, ffn2_w_gate, ffn2_w_up, ffn2_w_down, norm_final):
    p = dict(norm_ffn1=norm_ffn1, ffn1_w_gate=ffn1_w_gate, ffn1_w_up=ffn1_w_up, ffn1_w_down=ffn1_w_down,
             norm_mix=norm_mix, s5_w_in=s5_w_in, s5_lam_re=s5_lam_re, s5_lam_im=s5_lam_im,
             s5_log_step=s5_log_step, s5_b_re=s5_b_re, s5_b_im=s5_b_im, s5_c_re=s5_c_re, s5_c_im=s5_c_im,
             s5_d=s5_d, s5_w_glu=s5_w_glu, s5_w_out=s5_w_out, pool_w_in=pool_w_in, pool_w_group=pool_w_group,
             pool_scale=pool_scale, pool_w_out=pool_w_out, fox_w_in=fox_w_in, fox_b_f=fox_b_f,
             fox_w_out=fox_w_out, norm_ffn2=norm_ffn2, ffn2_w_gate=ffn2_w_gate, ffn2_w_up=ffn2_w_up,
             ffn2_w_down=ffn2_w_down, norm_final=norm_final)
    p = {k: (v.astype(BF16) if k in MATMUL_WEIGHTS else v) for k, v in p.items()}
    return jnp.stack([_trunk(x[b], p) for b in range(x.shape[0])])
```
